```python
import math
import jax, jax.numpy as jnp
from jax import lax
import numpy as np

D_MODEL = 1024
BATCH = 8
SEQ = 2048
DEPTH = 4

D_FF = ((8 * D_MODEL // 3 + 255) // 256) * 256
MIX_WIDTH = D_MODEL
SGU_WIDTH = MIX_WIDTH // 2
SGU_GROUP_DIM = 64
SGU_GROUPS = SGU_WIDTH // SGU_GROUP_DIM
CHUNK = 128
DIFF_WIDTH = MIX_WIDTH - SGU_WIDTH
DIFF_V_DIM = 128
DIFF_QK_DIM = DIFF_V_DIM // 2
DIFF_HEADS = DIFF_WIDTH // DIFF_V_DIM
IN_COLS = 2 * SGU_WIDTH + 3 * DIFF_WIDTH
Q_BLOCK = 128
EPS = 1e-6
NEG_INF = -1e30

kernel_name = "hybrid_sgu_diffattn_macaron"


def rmsnorm(x, gain):
    x32 = x.astype(jnp.float32)
    y = x32 * lax.rsqrt(jnp.mean(x32 * x32, axis=-1, keepdims=True) + EPS)
    return (y * gain.astype(jnp.float32)).astype(x.dtype)


def swiglu(h, w_gate, w_up, w_down):
    return (jax.nn.silu(h @ w_gate) * (h @ w_up)) @ w_down


def alibi_slopes(n_heads):
    i = jnp.arange(1, n_heads + 1, dtype=jnp.float32)
    return jnp.exp2(-8.0 * i / n_heads)


def spatial_gating(z, norm_gain, w_s, b_s):
    b, s, _ = z.shape
    nc = s // CHUNK
    u = z[..., :SGU_WIDTH].reshape(b, nc, CHUNK, SGU_GROUPS, SGU_GROUP_DIM)
    v = z[..., SGU_WIDTH:].reshape(b, s, SGU_GROUPS, SGU_GROUP_DIM)
    v = rmsnorm(v, norm_gain).reshape(b, nc, CHUNK, SGU_GROUPS, SGU_GROUP_DIM)
    causal = jnp.tril(jnp.ones((CHUNK, CHUNK), dtype=bool))
    w = jnp.where(causal[None], w_s, jnp.zeros_like(w_s))
    gate = jnp.einsum('gts,bcsgd->bctgd', w, v) + jnp.transpose(b_s)[None, None, :, :, None]
    return (u * gate).reshape(b, s, SGU_WIDTH)


def diff_attention(q, k, v, lam, slopes):
    b, s, h, _ = q.shape
    nb = s // Q_BLOCK
    scale = DIFF_QK_DIM ** -0.5
    qh = jnp.transpose(q, (0, 2, 1, 3)) * scale
    kh = jnp.transpose(k, (0, 2, 1, 3))
    vh = jnp.transpose(v, (0, 2, 1, 3))
    k1, k2 = kh[..., :DIFF_QK_DIM], kh[..., DIFF_QK_DIM:]
    qb = qh.reshape(b, h, nb, Q_BLOCK, 2 * DIFF_QK_DIM).transpose(2, 0, 1, 3, 4)
    kpos = jnp.arange(s)

    def block(args):
        qblk, i = args
        qpos = i * Q_BLOCK + jnp.arange(Q_BLOCK)
        dist = qpos[:, None] - kpos[None, :]
        bias = -slopes[:, None, None] * dist.astype(jnp.float32)[None]
        causal = dist >= 0

        def probs(qq, kk):
            sc = jnp.einsum('bhqd,bhkd->bhqk', qq, kk).astype(jnp.float32) + bias
            sc = jnp.where(causal, sc, NEG_INF)
            return jax.nn.softmax(sc, axis=-1)

        p = probs(qblk[..., :DIFF_QK_DIM], k1) - lam * probs(qblk[..., DIFF_QK_DIM:], k2)
        return jnp.einsum('bhqk,bhkd->bhqd', p.astype(vh.dtype), vh)

    o = lax.map(block, (qb, jnp.arange(nb)))
    return o.transpose(1, 0, 3, 2, 4).reshape(b, s, h, DIFF_V_DIM)


def setup_inputs(seed: int = 0) -> dict:
    key = jax.random.key(seed)
    ks = jax.random.split(key, 24)
    f32 = jnp.float32

    def nrm(k, shape, scale):
        return jax.random.normal(k, shape, f32) * scale

    def gain(k, shape):
        return 1.0 + 0.02 * jax.random.normal(k, shape, f32)

    return {
        "x": jax.random.normal(ks[0], (BATCH, SEQ, D_MODEL), f32),
        "ffn1_norm": gain(ks[1], (DEPTH, D_MODEL)),
        "ffn1_w_gate": nrm(ks[2], (DEPTH, D_MODEL, D_FF), D_MODEL ** -0.5),
        "ffn1_w_up": nrm(ks[3], (DEPTH, D_MODEL, D_FF), D_MODEL ** -0.5),
        "ffn1_w_down": nrm(ks[4], (DEPTH, D_FF, D_MODEL), D_FF ** -0.5),
        "mix_norm": gain(ks[5], (DEPTH, D_MODEL)),
        "w_in": nrm(ks[6], (DEPTH, D_MODEL, IN_COLS), D_MODEL ** -0.5),
        "sgu_norm": gain(ks[7], (DEPTH, SGU_GROUPS, SGU_GROUP_DIM)),
        "sgu_w": nrm(ks[8], (DEPTH, SGU_GROUPS, CHUNK, CHUNK), 0.5 * CHUNK ** -0.5),
        "sgu_b": 1.0 + 0.01 * jax.random.normal(ks[9], (DEPTH, SGU_GROUPS, CHUNK), f32),
        "lambda_q1": nrm(ks[10], (DEPTH, DIFF_QK_DIM), 0.1),
        "lambda_k1": nrm(ks[11], (DEPTH, DIFF_QK_DIM), 0.1),
        "lambda_q2": nrm(ks[12], (DEPTH, DIFF_QK_DIM), 0.1),
        "lambda_k2": nrm(ks[13], (DEPTH, DIFF_QK_DIM), 0.1),
        "diff_subln": gain(ks[14], (DEPTH, DIFF_V_DIM)),
        "w_out": nrm(ks[15], (DEPTH, MIX_WIDTH, D_MODEL), MIX_WIDTH ** -0.5),
        "ffn2_norm": gain(ks[16], (DEPTH, D_MODEL)),
        "ffn2_w_gate": nrm(ks[17], (DEPTH, D_MODEL, D_FF), D_MODEL ** -0.5),
        "ffn2_w_up": nrm(ks[18], (DEPTH, D_MODEL, D_FF), D_MODEL ** -0.5),
        "ffn2_w_down": nrm(ks[19], (DEPTH, D_FF, D_MODEL), D_FF ** -0.5),
        "final_norm": gain(ks[20], (D_MODEL,)),
    }


def reference(x, ffn1_norm, ffn1_w_gate, ffn1_w_up, ffn1_w_down, mix_norm, w_in,
              sgu_norm, sgu_w, sgu_b, lambda_q1, lambda_k1, lambda_q2, lambda_k2,
              diff_subln, w_out, ffn2_norm, ffn2_w_gate, ffn2_w_up, ffn2_w_down,
              final_norm):
    b, s, _ = x.shape
    slopes = alibi_slopes(DIFF_HEADS)
    for l in range(DEPTH):
        x = x + 0.5 * swiglu(rmsnorm(x, ffn1_norm[l]), ffn1_w_gate[l], ffn1_w_up[l], ffn1_w_down[l])

        h = rmsnorm(x, mix_norm[l])
        proj = h @ w_in[l]
        z_a = jax.nn.gelu(proj[..., :2 * SGU_WIDTH], approximate=False)
        y_a = spatial_gating(z_a, sgu_norm[l], sgu_w[l], sgu_b[l])

        off = 2 * SGU_WIDTH
        q = proj[..., off:off + DIFF_WIDTH].reshape(b, s, DIFF_HEADS, DIFF_V_DIM)
        k = proj[..., off + DIFF_WIDTH:off + 2 * DIFF_WIDTH].reshape(b, s, DIFF_HEADS, DIFF_V_DIM)
        v = proj[..., off + 2 * DIFF_WIDTH:off + 3 * DIFF_WIDTH].reshape(b, s, DIFF_HEADS, DIFF_V_DIM)
        lam_init = 0.8 - 0.6 * math.exp(-0.3 * l)
        lam = (jnp.exp(jnp.sum(lambda_q1[l].astype(jnp.float32) * lambda_k1[l].astype(jnp.float32)))
               - jnp.exp(jnp.sum(lambda_q2[l].astype(jnp.float32) * lambda_k2[l].astype(jnp.float32)))
               + lam_init)
        o = diff_attention(q, k, v, lam, slopes)
        o = rmsnorm(o, diff_subln[l]) * (1.0 - lam_init)
        y_b = o.reshape(b, s, DIFF_WIDTH)

        x = x + jnp.concatenate([y_a, y_b], axis=-1) @ w_out[l]

        x = x + 0.5 * swiglu(rmsnorm(x, ffn2_norm[l]), ffn2_w_gate[l], ffn2_w_up[l], ffn2_w_down[l])
    return rmsnorm(x, final_norm)
```

```python
import functools
import math

import jax
import jax.numpy as jnp
from jax import lax
from jax.experimental import pallas as pl
from jax.experimental.pallas import tpu as pltpu

D_MODEL = 1024
D_FF = 2816
SGU_WIDTH = 512
SGU_GROUP_DIM = 64
SGU_GROUPS = SGU_WIDTH // SGU_GROUP_DIM
CHUNK = 128
DIFF_WIDTH = 512
DIFF_V_DIM = 128
DIFF_QK_DIM = 64
DIFF_HEADS = DIFF_WIDTH // DIFF_V_DIM
IN_COLS = 2 * SGU_WIDTH + 3 * DIFF_WIDTH
EPS = 1e-6
NEG_INF = -1e30

VMEM_LIMIT_BYTES = 56 * 1024 * 1024

FFN_TOKENS = 512
FFN_CHUNKS = ((0, 1536), (1536, 2816))
INPROJ_TOKENS = 512
ATTN_Q = 256
ATTN_K = 256

bf16 = jnp.bfloat16
f32 = jnp.float32


def _rmsnorm_rows(x, gain):
    ms = jnp.mean(x * x, axis=-1, keepdims=True)
    return x * lax.rsqrt(ms + EPS) * gain


def _dot(a, b):
    return jnp.dot(a, b, preferred_element_type=f32)


def _dot_nt(a, b):
    return lax.dot_general(a, b, (((1,), (1,)), ((), ())), preferred_element_type=f32)


def _swiglu_half_step(x, gain, wg_ref, wu_ref, wd_ref):
    h = _rmsnorm_rows(x, gain).astype(bf16)
    acc = None
    for c0, c1 in FFN_CHUNKS:
        g = _dot(h, wg_ref[:, c0:c1])
        u = _dot(h, wu_ref[:, c0:c1])
        a = (g / (1.0 + jnp.exp(-g)) * u).astype(bf16)
        d = _dot(a, wd_ref[c0:c1, :])
        acc = d if acc is None else acc + d
    return x + 0.5 * acc


def _ffn_kernel(x_ref, gain_ref, wg_ref, wu_ref, wd_ref, o_ref):
    o_ref[...] = _swiglu_half_step(x_ref[...], gain_ref[...], wg_ref, wu_ref, wd_ref)


def _mix_ffn_kernel(x_ref, ya_ref, yb_ref, wo_ref, gain_ref, wg_ref, wu_ref, wd_ref,
                    fgain_ref, o_ref, *, final_norm):
    x = x_ref[...]
    x = x + _dot(ya_ref[...], wo_ref[:SGU_WIDTH, :]) + _dot(yb_ref[...], wo_ref[SGU_WIDTH:, :])
    x = _swiglu_half_step(x, gain_ref[...], wg_ref, wu_ref, wd_ref)
    if final_norm:
        x = _rmsnorm_rows(x, fgain_ref[...])
    o_ref[...] = x


def _const_spec(shape):
    nd = len(shape)
    return pl.BlockSpec(shape, lambda *_: (0,) * nd, pipeline_mode=pl.Buffered(1))


def _ffn_call(x2d, gain, wg, wu, wd):
    t = x2d.shape[0]
    tm = FFN_TOKENS
    row_spec = pl.BlockSpec((tm, D_MODEL), lambda i: (i, 0))
    return pl.pallas_call(
        _ffn_kernel,
        grid=(t // tm,),
        in_specs=[row_spec, _const_spec((1, D_MODEL)), _const_spec((D_MODEL, D_FF)),
                  _const_spec((D_MODEL, D_FF)), _const_spec((D_FF, D_MODEL))],
        out_specs=row_spec,
        out_shape=jax.ShapeDtypeStruct((t, D_MODEL), f32),
        compiler_params=pltpu.CompilerParams(
            dimension_semantics=("parallel",), vmem_limit_bytes=VMEM_LIMIT_BYTES),
        name="ffn",
    )(x2d, gain, wg, wu, wd)


def _mix_ffn_call(x2d, ya, yb, wo, gain, wg, wu, wd, fgain, final_norm):
    t = x2d.shape[0]
    tm = FFN_TOKENS
    row_spec = pl.BlockSpec((tm, D_MODEL), lambda i: (i, 0))
    half_spec = pl.BlockSpec((tm, SGU_WIDTH), lambda i: (i, 0))
    return pl.pallas_call(
        functools.partial(_mix_ffn_kernel, final_norm=final_norm),
        grid=(t // tm,),
        in_specs=[row_spec, half_spec, half_spec, _const_spec((2 * SGU_WIDTH, D_MODEL)),
                  _const_spec((1, D_MODEL)), _const_spec((D_MODEL, D_FF)),
                  _const_spec((D_MODEL, D_FF)), _const_spec((D_FF, D_MODEL)),
                  _const_spec((1, D_MODEL))],
        out_specs=row_spec,
        out_shape=jax.ShapeDtypeStruct((t, D_MODEL), f32),
        compiler_params=pltpu.CompilerParams(
            dimension_semantics=("parallel",), vmem_limit_bytes=VMEM_LIMIT_BYTES),
        name="mix_ffn",
    )(x2d, ya, yb, wo, gain, wg, wu, wd, fgain)


def _gelu_exact(x):
    return 0.5 * x * (1.0 + lax.erf(x * (1.0 / math.sqrt(2.0))))


def _inproj_kernel(x_ref, gain_ref, win_ref, ones_ref, sgain_ref, ws_ref, bs_ref,
                   ya_ref, q_ref, k_ref, v_ref):
    tm = x_ref.shape[0]
    h = _rmsnorm_rows(x_ref[...], gain_ref[...]).astype(bf16)

    qkv = _dot(h, win_ref[:, 2 * SGU_WIDTH:])
    q_ref[...] = (qkv[:, :DIFF_WIDTH] * (DIFF_QK_DIM ** -0.5)).astype(bf16)
    k_ref[...] = qkv[:, DIFF_WIDTH:2 * DIFF_WIDTH].astype(bf16)
    v_ref[...] = qkv[:, 2 * DIFF_WIDTH:].astype(bf16)

    z = _gelu_exact(_dot(h, win_ref[:, :2 * SGU_WIDTH]))
    u = z[:, :SGU_WIDTH]
    v = z[:, SGU_WIDTH:]
    v2 = v * v
    v2_hi = v2.astype(bf16)
    v2_lo = (v2 - v2_hi.astype(f32)).astype(bf16)
    ones = ones_ref[...]
    ms = jnp.concatenate(
        [_dot(v2_hi[:, c:c + 256], ones) + _dot(v2_lo[:, c:c + 256], ones)
         for c in range(0, SGU_WIDTH, 256)], axis=-1) * (1.0 / SGU_GROUP_DIM)
    vn = (v * lax.rsqrt(ms + EPS) * sgain_ref[...]).astype(bf16)

    row = lax.broadcasted_iota(jnp.int32, (CHUNK, CHUNK), 0)
    col = lax.broadcasted_iota(jnp.int32, (CHUNK, CHUNK), 1)
    causal = col <= row
    for g in range(SGU_GROUPS):
        w = jnp.where(causal, ws_ref[g], 0.0).astype(bf16)
        bias = bs_ref[g]
        lo, hi = g * SGU_GROUP_DIM, (g + 1) * SGU_GROUP_DIM
        for c in range(tm // CHUNK):
            r0, r1 = c * CHUNK, (c + 1) * CHUNK
            gate = _dot(w, vn[r0:r1, lo:hi]) + bias
            ya_ref[r0:r1, lo:hi] = (u[r0:r1, lo:hi] * gate).astype(bf16)


def _inproj_call(x2d, gain, win, ones, sgain, ws, bs):
    t = x2d.shape[0]
    tm = INPROJ_TOKENS
    row_spec = pl.BlockSpec((tm, D_MODEL), lambda i: (i, 0))
    half_spec = pl.BlockSpec((tm, SGU_WIDTH), lambda i: (i, 0))
    half_shape = jax.ShapeDtypeStruct((t, SGU_WIDTH), bf16)
    return pl.pallas_call(
        _inproj_kernel,
        grid=(t // tm,),
        in_specs=[row_spec, _const_spec((1, D_MODEL)), _const_spec((D_MODEL, IN_COLS)),
                  _const_spec((256, 256)), _const_spec((1, SGU_WIDTH)),
                  _const_spec((SGU_GROUPS, CHUNK, CHUNK)), _const_spec((SGU_GROUPS, CHUNK, 1))],
        out_specs=[half_spec] * 4,
        out_shape=[half_shape] * 4,
        compiler_params=pltpu.CompilerParams(
            dimension_semantics=("parallel",), vmem_limit_bytes=VMEM_LIMIT_BYTES),
        name="inproj_sgu",
    )(x2d, gain, win, ones, sgain, ws, bs)


def _attn_kernel(lq1_ref, lk1_ref, lq2_ref, lk2_ref, gain_ref, q_ref, k_ref, v_ref, o_ref,
                 m_ref, l_ref, acc_ref, *, lam_init):
    tq, tk = ATTN_Q, ATTN_K
    head = pl.program_id(1)
    qi = pl.program_id(2)
    slope = jnp.where(head == 0, 0.25, jnp.where(head == 1, 0.0625,
                      jnp.where(head == 2, 0.015625, 0.00390625))).astype(f32)

    q = q_ref[0]
    qs = (q[:, :DIFF_QK_DIM], q[:, DIFF_QK_DIM:])
    m_ref[...] = jnp.full(m_ref.shape, NEG_INF, f32)
    l_ref[...] = jnp.zeros(l_ref.shape, f32)
    acc_ref[...] = jnp.zeros(acc_ref.shape, f32)

    dist = (lax.broadcasted_iota(jnp.int32, (tq, tk), 0)
            - lax.broadcasted_iota(jnp.int32, (tq, tk), 1))
    dist_f = dist.astype(f32)

    def step(kj, masked):
        k0 = pl.multiple_of(kj * tk, tk)
        kblk = k_ref[0, pl.ds(k0, tk), :]
        vblk = v_ref[0, pl.ds(k0, tk), :]
        offset = ((qi * tq - kj * tk)).astype(f32)
        bias = -slope * (dist_f + offset)
        for a in range(2):
            kk = kblk[:, a * DIFF_QK_DIM:(a + 1) * DIFF_QK_DIM]
            s = _dot_nt(qs[a], kk) + bias
            if masked:
                s = jnp.where(dist >= 0, s, NEG_INF)
            m_old = m_ref[a]
            m_new = jnp.maximum(m_old, jnp.max(s, axis=-1, keepdims=True))
            alpha = jnp.exp(m_old - m_new)
            p = jnp.exp(s - m_new)
            l_ref[a] = alpha * l_ref[a] + jnp.sum(p, axis=-1, keepdims=True)
            acc_ref[a] = alpha * acc_ref[a] + _dot(p.astype(bf16), vblk)
            m_ref[a] = m_new

    def body(kj, carry):
        step(kj, False)
        return carry

    lax.fori_loop(0, qi, body, 0)
    step(qi, True)

    lam = (jnp.exp(jnp.sum(lq1_ref[...] * lk1_ref[...], axis=-1, keepdims=True))
           - jnp.exp(jnp.sum(lq2_ref[...] * lk2_ref[...], axis=-1, keepdims=True))
           + lam_init)
    o = acc_ref[0] / l_ref[0] - lam * (acc_ref[1] / l_ref[1])
    o = _rmsnorm_rows(o, gain_ref[...]) * (1.0 - lam_init)
    o_ref[0] = o.astype(bf16)


def _attn_call(lq1, lk1, lq2, lk2, gain, q, k, v, lam_init):
    b, s, _ = q.shape
    tq = ATTN_Q
    lam_spec = _const_spec((1, DIFF_QK_DIM))
    q_spec = pl.BlockSpec((1, tq, DIFF_V_DIM), lambda bi, hi, qi: (bi, qi, hi))
    kv_spec = pl.BlockSpec((1, s, DIFF_V_DIM), lambda bi, hi, qi: (bi, 0, hi))
    return pl.pallas_call(
        functools.partial(_attn_kernel, lam_init=lam_init),
        grid=(b, DIFF_HEADS, s // tq),
        in_specs=[lam_spec, lam_spec, lam_spec, lam_spec, _const_spec((1, DIFF_V_DIM)),
                  q_spec, kv_spec, kv_spec],
        out_specs=q_spec,
        out_shape=jax.ShapeDtypeStruct((b, s, DIFF_WIDTH), bf16),
        scratch_shapes=[pltpu.VMEM((2, tq, 1), f32), pltpu.VMEM((2, tq, 1), f32),
                        pltpu.VMEM((2, tq, DIFF_V_DIM), f32)],
        compiler_params=pltpu.CompilerParams(
            dimension_semantics=("parallel", "parallel", "arbitrary"),
            vmem_limit_bytes=VMEM_LIMIT_BYTES),
        name="diff_attn",
    )(lq1, lk1, lq2, lk2, gain, q, k, v)


def kernel(x, ffn1_norm, ffn1_w_gate, ffn1_w_up, ffn1_w_down, mix_norm, w_in, sgu_norm, sgu_w, sgu_b, lambda_q1, lambda_k1, lambda_q2, lambda_k2, diff_subln, w_out, ffn2_norm, ffn2_w_gate, ffn2_w_up, ffn2_w_down, final_norm):
    b, s, d = x.shape
    depth = w_in.shape[0]
    t = b * s
    x2d = x.reshape(t, d)
    group_id = jnp.arange(256) // SGU_GROUP_DIM
    ones = (group_id[:, None] == group_id[None, :]).astype(bf16)
    fgain = final_norm.reshape(1, d)

    for l in range(depth):
        x2d = _ffn_call(x2d, ffn1_norm[l].reshape(1, d), ffn1_w_gate[l].astype(bf16),
                        ffn1_w_up[l].astype(bf16), ffn1_w_down[l].astype(bf16))
        ya, q, k, v = _inproj_call(
            x2d, mix_norm[l].reshape(1, d), w_in[l].astype(bf16), ones,
            sgu_norm[l].reshape(1, SGU_WIDTH), sgu_w[l], sgu_b[l].reshape(SGU_GROUPS, CHUNK, 1))
        lam_init = 0.8 - 0.6 * math.exp(-0.3 * l)
        yb = _attn_call(lambda_q1[l].reshape(1, -1), lambda_k1[l].reshape(1, -1),
                        lambda_q2[l].reshape(1, -1), lambda_k2[l].reshape(1, -1),
                        diff_subln[l].reshape(1, -1),
                        q.reshape(b, s, DIFF_WIDTH), k.reshape(b, s, DIFF_WIDTH),
                        v.reshape(b, s, DIFF_WIDTH), lam_init)
        x2d = _mix_ffn_call(x2d, ya, yb.reshape(t, DIFF_WIDTH), w_out[l].astype(bf16),
                            ffn2_norm[l].reshape(1, d), ffn2_w_gate[l].astype(bf16),
                            ffn2_w_up[l].astype(bf16), ffn2_w_down[l].astype(bf16),
                            fgain, final_norm=(l == depth - 1))
    return x2d.reshape(b, s, d)
```

```python
import functools
import math

import jax
import jax.numpy as jnp
from jax import lax
from jax.experimental import pallas as pl
from jax.experimental.pallas import tpu as pltpu

D_MODEL = 1024
D_FF = 2816
SGU_WIDTH = 512
SGU_GROUP_DIM = 64
SGU_GROUPS = SGU_WIDTH // SGU_GROUP_DIM
CHUNK = 128
DIFF_WIDTH = 512
DIFF_V_DIM = 128
DIFF_QK_DIM = 64
DIFF_HEADS = DIFF_WIDTH // DIFF_V_DIM
IN_COLS = 2 * SGU_WIDTH + 3 * DIFF_WIDTH
EPS = 1e-6
NEG_INF = -1e30

VMEM_LIMIT_BYTES = 56 * 1024 * 1024

FFN_TOKENS = 512
FFN_CHUNKS = ((0, 1536), (1536, 2816))
INPROJ_TOKENS = 512
ATTN_BLOCK = 256
VT_ROWS = DIFF_V_DIM + 16

bf16 = jnp.bfloat16
f32 = jnp.float32


def _rmsnorm_rows(x, gain):
    ms = jnp.mean(x * x, axis=-1, keepdims=True)
    return x * lax.rsqrt(ms + EPS) * gain


def _dot(a, b):
    return jnp.dot(a, b, preferred_element_type=f32)


def _dot_nt(a, b):
    return lax.dot_general(a, b, (((1,), (1,)), ((), ())), preferred_element_type=f32)


def _swiglu_half_step(x, gain, wg_ref, wu_ref, wd_ref):
    h = _rmsnorm_rows(x, gain).astype(bf16)
    acc = None
    for c0, c1 in FFN_CHUNKS:
        g = _dot(h, wg_ref[:, c0:c1])
        u = _dot(h, wu_ref[:, c0:c1])
        a = (g / (1.0 + jnp.exp(-g)) * u).astype(bf16)
        d = _dot(a, wd_ref[c0:c1, :])
        acc = d if acc is None else acc + d
    return x + 0.5 * acc


def _ffn_kernel(x_ref, gain_ref, wg_ref, wu_ref, wd_ref, o_ref):
    o_ref[...] = _swiglu_half_step(x_ref[...], gain_ref[...], wg_ref, wu_ref, wd_ref)


def _mix_ffn_kernel(x_ref, ya_ref, yb_ref, wo_ref, gain_ref, wg_ref, wu_ref, wd_ref,
                    fgain_ref, o_ref, *, final_norm):
    x = x_ref[...]
    x = x + _dot(ya_ref[...], wo_ref[:SGU_WIDTH, :]) + _dot(yb_ref[...], wo_ref[SGU_WIDTH:, :])
    x = _swiglu_half_step(x, gain_ref[...], wg_ref, wu_ref, wd_ref)
    if final_norm:
        x = _rmsnorm_rows(x, fgain_ref[...])
    o_ref[...] = x


def _const_spec(shape):
    nd = len(shape)
    return pl.BlockSpec(shape, lambda *_: (0,) * nd, pipeline_mode=pl.Buffered(1))


def _layer_spec(layer, shape):
    nd = len(shape)
    return pl.BlockSpec((None,) + tuple(shape), lambda *_: (layer,) + (0,) * nd,
                        pipeline_mode=pl.Buffered(1))


def _ffn_call(layer, x2d, gain, wg, wu, wd):
    t = x2d.shape[0]
    tm = FFN_TOKENS
    row_spec = pl.BlockSpec((tm, D_MODEL), lambda i: (i, 0))
    return pl.pallas_call(
        _ffn_kernel,
        grid=(t // tm,),
        in_specs=[row_spec, _layer_spec(layer, (1, D_MODEL)), _layer_spec(layer, (D_MODEL, D_FF)),
                  _layer_spec(layer, (D_MODEL, D_FF)), _layer_spec(layer, (D_FF, D_MODEL))],
        out_specs=row_spec,
        out_shape=jax.ShapeDtypeStruct((t, D_MODEL), f32),
        compiler_params=pltpu.CompilerParams(
            dimension_semantics=("parallel",), vmem_limit_bytes=VMEM_LIMIT_BYTES),
        name="ffn",
    )(x2d, gain, wg, wu, wd)


def _mix_ffn_call(layer, x2d, ya, yb, wo, gain, wg, wu, wd, fgain, final_norm):
    t = x2d.shape[0]
    tm = FFN_TOKENS
    row_spec = pl.BlockSpec((tm, D_MODEL), lambda i: (i, 0))
    half_spec = pl.BlockSpec((tm, SGU_WIDTH), lambda i: (i, 0))
    return pl.pallas_call(
        functools.partial(_mix_ffn_kernel, final_norm=final_norm),
        grid=(t // tm,),
        in_specs=[row_spec, half_spec, half_spec, _layer_spec(layer, (2 * SGU_WIDTH, D_MODEL)),
                  _layer_spec(layer, (1, D_MODEL)), _layer_spec(layer, (D_MODEL, D_FF)),
                  _layer_spec(layer, (D_MODEL, D_FF)), _layer_spec(layer, (D_FF, D_MODEL)),
                  _const_spec((1, D_MODEL))],
        out_specs=row_spec,
        out_shape=jax.ShapeDtypeStruct((t, D_MODEL), f32),
        compiler_params=pltpu.CompilerParams(
            dimension_semantics=("parallel",), vmem_limit_bytes=VMEM_LIMIT_BYTES),
        name="mix_ffn",
    )(x2d, ya, yb, wo, gain, wg, wu, wd, fgain)


def _gelu_exact(x):
    return 0.5 * x * (1.0 + lax.erf(x * (1.0 / math.sqrt(2.0))))


def _inproj_kernel(x_ref, gain_ref, win_ref, wvt_ref, ones_ref, sgain_ref, ws_ref, bs_ref,
                   ya_ref, q_ref, k_ref, vt_ref):
    tm = x_ref.shape[0]
    h = _rmsnorm_rows(x_ref[...], gain_ref[...]).astype(bf16)

    qk = _dot(h, win_ref[:, 2 * SGU_WIDTH:2 * SGU_WIDTH + 2 * DIFF_WIDTH])
    q_ref[...] = (qk[:, :DIFF_WIDTH] * (DIFF_QK_DIM ** -0.5)).astype(bf16)
    k_ref[...] = qk[:, DIFF_WIDTH:].astype(bf16)

    vt = _dot_nt(wvt_ref[...], h).astype(bf16)
    ones_rows = jnp.ones((VT_ROWS - DIFF_V_DIM, ATTN_BLOCK), bf16)
    for hd in range(DIFF_HEADS):
        for c in range(tm // ATTN_BLOCK):
            vt_ref[0, hd, c, :DIFF_V_DIM, :] = vt[hd * DIFF_V_DIM:(hd + 1) * DIFF_V_DIM,
                                                  c * ATTN_BLOCK:(c + 1) * ATTN_BLOCK]
            vt_ref[0, hd, c, DIFF_V_DIM:, :] = ones_rows

    z = _gelu_exact(_dot(h, win_ref[:, :2 * SGU_WIDTH]))
    u = z[:, :SGU_WIDTH]
    v = z[:, SGU_WIDTH:]
    v2 = v * v
    v2_hi = v2.astype(bf16)
    v2_lo = (v2 - v2_hi.astype(f32)).astype(bf16)
    ones = ones_ref[...]
    ms = jnp.concatenate(
        [_dot(v2_hi[:, c:c + 256], ones) + _dot(v2_lo[:, c:c + 256], ones)
         for c in range(0, SGU_WIDTH, 256)], axis=-1) * (1.0 / SGU_GROUP_DIM)
    vn = (v * lax.rsqrt(ms + EPS) * sgain_ref[...]).astype(bf16)

    row = lax.broadcasted_iota(jnp.int32, (CHUNK, CHUNK), 0)
    col = lax.broadcasted_iota(jnp.int32, (CHUNK, CHUNK), 1)
    causal = col <= row
    for g in range(SGU_GROUPS):
        w = jnp.where(causal, ws_ref[g], 0.0).astype(bf16)
        bias = bs_ref[g]
        lo, hi = g * SGU_GROUP_DIM, (g + 1) * SGU_GROUP_DIM
        for c in range(tm // CHUNK):
            r0, r1 = c * CHUNK, (c + 1) * CHUNK
            gate = _dot(w, vn[r0:r1, lo:hi]) + bias
            ya_ref[r0:r1, lo:hi] = (u[r0:r1, lo:hi] * gate).astype(bf16)


def _inproj_call(layer, x2d, seq, gain, win, wvt, ones, sgain, ws, bs):
    t = x2d.shape[0]
    tm = INPROJ_TOKENS
    tiles_per_seq = seq // tm
    row_spec = pl.BlockSpec((tm, D_MODEL), lambda i: (i, 0))
    half_spec = pl.BlockSpec((tm, SGU_WIDTH), lambda i: (i, 0))
    half_shape = jax.ShapeDtypeStruct((t, SGU_WIDTH), bf16)
    vt_spec = pl.BlockSpec((1, DIFF_HEADS, tm // ATTN_BLOCK, VT_ROWS, ATTN_BLOCK),
                           lambda i: (i // tiles_per_seq, 0, i % tiles_per_seq, 0, 0))
    vt_shape = jax.ShapeDtypeStruct(
        (t // seq, DIFF_HEADS, seq // ATTN_BLOCK, VT_ROWS, ATTN_BLOCK), bf16)
    return pl.pallas_call(
        _inproj_kernel,
        grid=(t // tm,),
        in_specs=[row_spec, _layer_spec(layer, (1, D_MODEL)),
                  _layer_spec(layer, (D_MODEL, IN_COLS)),
                  _layer_spec(layer, (DIFF_WIDTH, D_MODEL)),
                  _const_spec((256, 256)), _layer_spec(layer, (1, SGU_WIDTH)),
                  _layer_spec(layer, (SGU_GROUPS, CHUNK, CHUNK)),
                  _layer_spec(layer, (SGU_GROUPS, CHUNK, 1))],
        out_specs=[half_spec, half_spec, half_spec, vt_spec],
        out_shape=[half_shape, half_shape, half_shape, vt_shape],
        compiler_params=pltpu.CompilerParams(
            dimension_semantics=("parallel",), vmem_limit_bytes=VMEM_LIMIT_BYTES),
        name="inproj_sgu",
    )(x2d, gain, win, wvt, ones, sgain, ws, bs)


def _attn_kernel(lq1_ref, lk1_ref, lq2_ref, lk2_ref, gain_ref, kbias_ref, q_ref, k_ref, vt_ref,
                 o_ref, m_ref, acc_ref, *, lam_init):
    blk = ATTN_BLOCK
    qi = pl.program_id(1)
    heads = range(DIFF_HEADS)
    slopes = [2.0 ** (-8.0 * (hd + 1) / DIFF_HEADS) for hd in heads]

    lane = lax.broadcasted_iota(jnp.int32, (blk, DIFF_V_DIM), 1)
    q_tail = jnp.where(lane < 1, 1.0, 0.0).astype(bf16)
    q_ops = []
    for hd in heads:
        q = q_ref[0, :, hd * DIFF_V_DIM:(hd + 1) * DIFF_V_DIM].astype(f32)
        q_ops.append(jnp.concatenate(
            [jnp.concatenate([jnp.where(lane < DIFF_QK_DIM, q, 0.0).astype(bf16), q_tail], axis=1),
             jnp.concatenate([jnp.where(lane >= DIFF_QK_DIM, q, 0.0).astype(bf16), q_tail], axis=1)],
            axis=0))

    m_ref[...] = jnp.full(m_ref.shape, NEG_INF, f32)
    acc_ref[...] = jnp.zeros(acc_ref.shape, f32)

    def scores(kb):
        k_rows = k_ref[0, pl.ds(pl.multiple_of(kb * blk, blk), blk), :]
        return tuple(
            _dot_nt(jnp.concatenate([k_rows[:, hd * DIFF_V_DIM:(hd + 1) * DIFF_V_DIM],
                                     kbias_ref[hd]], axis=1), q_ops[hd])
            for hd in heads)

    def consume(s_all, kb, masked):
        for hd in heads:
            s = s_all[hd]
            beta = slopes[hd] * ((kb - qi) * blk).astype(f32)
            if masked:
                krow = lax.broadcasted_iota(jnp.int32, s.shape, 0)
                qcol = lax.broadcasted_iota(jnp.int32, s.shape, 1) & (blk - 1)
                s = jnp.where(krow <= qcol, s, NEG_INF)
            m_old = m_ref[hd]
            m_new = jnp.maximum(m_old, jnp.max(s, axis=0, keepdims=True) + beta)
            alpha = jnp.exp(m_old - m_new)
            p = jnp.exp(s - (m_new - beta)).astype(bf16)
            acc_ref[hd] = alpha * acc_ref[hd] + _dot(vt_ref[0, hd, kb], p)
            m_ref[hd] = m_new

    def body(kb, carry):
        consume(scores(kb), kb, False)
        return carry

    lax.fori_loop(0, qi, body, 0)
    consume(scores(qi), qi, True)

    lam = (jnp.exp(jnp.sum(lq1_ref[...] * lk1_ref[...], axis=-1, keepdims=True))
           - jnp.exp(jnp.sum(lq2_ref[...] * lk2_ref[...], axis=-1, keepdims=True))
           + lam_init)
    gain = gain_ref[...] * (1.0 - lam_init)
    outs = []
    for hd in heads:
        acc1, acc2 = acc_ref[hd, :, :blk], acc_ref[hd, :, blk:]
        o = (acc1[:DIFF_V_DIM] / acc1[DIFF_V_DIM:DIFF_V_DIM + 1]
             - lam * (acc2[:DIFF_V_DIM] / acc2[DIFF_V_DIM:DIFF_V_DIM + 1]))
        ms = jnp.mean(o * o, axis=0, keepdims=True)
        outs.append((o * lax.rsqrt(ms + EPS) * gain).T.astype(bf16))
    o_ref[0] = jnp.concatenate(outs, axis=1)


def _attn_call(layer, lq1, lk1, lq2, lk2, gain, kbias, q, k, vt, lam_init):
    b, s, _ = q.shape
    blk = ATTN_BLOCK
    lam_spec = _layer_spec(layer, (1, DIFF_QK_DIM))
    q_spec = pl.BlockSpec((1, blk, DIFF_WIDTH), lambda bi, qi: (bi, qi, 0))
    k_spec = pl.BlockSpec((1, s, DIFF_WIDTH), lambda bi, qi: (bi, 0, 0))
    vt_spec = pl.BlockSpec((1, DIFF_HEADS, s // blk, VT_ROWS, blk),
                           lambda bi, qi: (bi, 0, 0, 0, 0))
    return pl.pallas_call(
        functools.partial(_attn_kernel, lam_init=lam_init),
        grid=(b, s // blk),
        in_specs=[lam_spec, lam_spec, lam_spec, lam_spec, _layer_spec(layer, (DIFF_V_DIM, 1)),
                  _const_spec((DIFF_HEADS, blk, DIFF_V_DIM)), q_spec, k_spec, vt_spec],
        out_specs=q_spec,
        out_shape=jax.ShapeDtypeStruct((b, s, DIFF_WIDTH), bf16),
        scratch_shapes=[pltpu.VMEM((DIFF_HEADS, 1, 2 * blk), f32),
                        pltpu.VMEM((DIFF_HEADS, VT_ROWS, 2 * blk), f32)],
        compiler_params=pltpu.CompilerParams(
            dimension_semantics=("parallel", "arbitrary"),
            vmem_limit_bytes=VMEM_LIMIT_BYTES),
        name="diff_attn",
    )(lq1, lk1, lq2, lk2, gain, kbias, q, k, vt)


def _alibi_key_columns():
    r = jnp.arange(ATTN_BLOCK, dtype=f32)
    slopes = jnp.exp2(-8.0 * jnp.arange(1, DIFF_HEADS + 1, dtype=f32) / DIFF_HEADS)
    cols = jnp.zeros((DIFF_HEADS, ATTN_BLOCK, DIFF_V_DIM), f32)
    cols = cols.at[:, :, 0].set(slopes[:, None] * r[None, :])
    return cols.astype(bf16)


def kernel(x, ffn1_norm, ffn1_w_gate, ffn1_w_up, ffn1_w_down, mix_norm, w_in, sgu_norm, sgu_w, sgu_b, lambda_q1, lambda_k1, lambda_q2, lambda_k2, diff_subln, w_out, ffn2_norm, ffn2_w_gate, ffn2_w_up, ffn2_w_down, final_norm):
    b, s, d = x.shape
    depth = w_in.shape[0]
    t = b * s
    x2d = x.reshape(t, d)
    group_id = jnp.arange(256) // SGU_GROUP_DIM
    ones = (group_id[:, None] == group_id[None, :]).astype(bf16)
    kbias = _alibi_key_columns()

    as_rows = lambda p: p.reshape(depth, 1, -1)
    wg1, wu1, wd1 = ffn1_w_gate.astype(bf16), ffn1_w_up.astype(bf16), ffn1_w_down.astype(bf16)
    wg2, wu2, wd2 = ffn2_w_gate.astype(bf16), ffn2_w_up.astype(bf16), ffn2_w_down.astype(bf16)
    win = w_in.astype(bf16)
    wvt = jnp.swapaxes(w_in[:, :, IN_COLS - DIFF_WIDTH:], 1, 2).astype(bf16)
    wo = w_out.astype(bf16)
    sgu_b_cols = sgu_b.reshape(depth, SGU_GROUPS, CHUNK, 1)
    subln_cols = diff_subln.reshape(depth, DIFF_V_DIM, 1)

    for l in range(depth):
        x2d = _ffn_call(l, x2d, as_rows(ffn1_norm), wg1, wu1, wd1)
        ya, q, k, vt = _inproj_call(l, x2d, s, as_rows(mix_norm), win, wvt, ones,
                                    as_rows(sgu_norm), sgu_w, sgu_b_cols)
        lam_init = 0.8 - 0.6 * math.exp(-0.3 * l)
        yb = _attn_call(l, as_rows(lambda_q1), as_rows(lambda_k1), as_rows(lambda_q2),
                        as_rows(lambda_k2), subln_cols, kbias,
                        q.reshape(b, s, DIFF_WIDTH), k.reshape(b, s, DIFF_WIDTH), vt, lam_init)
        x2d = _mix_ffn_call(l, x2d, ya, yb.reshape(t, DIFF_WIDTH), wo, as_rows(ffn2_norm),
                            wg2, wu2, wd2, final_norm.reshape(1, d),
                            final_norm=(l == depth - 1))
    return x2d.reshape(b, s, d)
```

```python
import functools
import math

import jax
import jax.numpy as jnp
from jax import lax
from jax.experimental import pallas as pl
from jax.experimental.pallas import tpu as pltpu

D_MODEL = 1024
D_FF = 2816
SGU_WIDTH = 512
SGU_GROUP_DIM = 64
SGU_GROUPS = SGU_WIDTH // SGU_GROUP_DIM
CHUNK = 128
DIFF_WIDTH = 512
DIFF_V_DIM = 128
DIFF_QK_DIM = 64
DIFF_HEADS = DIFF_WIDTH // DIFF_V_DIM
IN_COLS = 2 * SGU_WIDTH + 3 * DIFF_WIDTH
EPS = 1e-6
NEG_INF = -1e30
LOG2_E = math.log2(math.e)

VMEM_LIMIT_BYTES = 56 * 1024 * 1024

FFN_TOKENS = 512
FFN_CHUNKS = ((0, 1536), (1536, 2816))
INPROJ_TOKENS = 512
ATTN_BLOCK = 256
VT_ROWS = DIFF_V_DIM + 16
ATTN_LOOKAHEAD = 3
ATTN_PV_DELAY = 2

bf16 = jnp.bfloat16
f32 = jnp.float32


def _rmsnorm_rows(x, gain):
    ms = jnp.mean(x * x, axis=-1, keepdims=True)
    return x * lax.rsqrt(ms + EPS) * gain


def _dot(a, b):
    return jnp.dot(a, b, preferred_element_type=f32)


def _dot_nt(a, b):
    return lax.dot_general(a, b, (((1,), (1,)), ((), ())), preferred_element_type=f32)


def _swiglu_half_step(x, gain, wg_ref, wu_ref, wd_ref):
    h = _rmsnorm_rows(x, gain).astype(bf16)
    acc = None
    for c0, c1 in FFN_CHUNKS:
        g = _dot(h, wg_ref[:, c0:c1])
        u = _dot(h, wu_ref[:, c0:c1])
        a = (g / (1.0 + jnp.exp(-g)) * u).astype(bf16)
        d = _dot(a, wd_ref[c0:c1, :])
        acc = d if acc is None else acc + d
    return x + 0.5 * acc


def _ffn_kernel(x_ref, gain_ref, wg_ref, wu_ref, wd_ref, o_ref):
    o_ref[...] = _swiglu_half_step(x_ref[...], gain_ref[...], wg_ref, wu_ref, wd_ref)


def _dot_tn(a, b):
    return lax.dot_general(a, b, (((0,), (0,)), ((), ())), preferred_element_type=f32)


def _mix_ffn_kernel(x_ref, ya_ref, ybt_ref, wo_ref, gain_ref, wg_ref, wu_ref, wd_ref,
                    fgain_ref, o_ref, *, final_norm):
    x = x_ref[...]
    x = (x + _dot(ya_ref[...], wo_ref[:SGU_WIDTH, :])
         + _dot_tn(ybt_ref[0], wo_ref[SGU_WIDTH:, :]))
    x = _swiglu_half_step(x, gain_ref[...], wg_ref, wu_ref, wd_ref)
    if final_norm:
        x = _rmsnorm_rows(x, fgain_ref[...])
    o_ref[...] = x


def _const_spec(shape):
    nd = len(shape)
    return pl.BlockSpec(shape, lambda *_: (0,) * nd, pipeline_mode=pl.Buffered(1))


def _layer_spec(layer, shape):
    nd = len(shape)
    return pl.BlockSpec((None,) + tuple(shape), lambda *_: (layer,) + (0,) * nd,
                        pipeline_mode=pl.Buffered(1))


def _ffn_call(layer, x2d, gain, wg, wu, wd):
    t = x2d.shape[0]
    tm = FFN_TOKENS
    row_spec = pl.BlockSpec((tm, D_MODEL), lambda i: (i, 0))
    return pl.pallas_call(
        _ffn_kernel,
        grid=(t // tm,),
        in_specs=[row_spec, _layer_spec(layer, (1, D_MODEL)), _layer_spec(layer, (D_MODEL, D_FF)),
                  _layer_spec(layer, (D_MODEL, D_FF)), _layer_spec(layer, (D_FF, D_MODEL))],
        out_specs=row_spec,
        out_shape=jax.ShapeDtypeStruct((t, D_MODEL), f32),
        compiler_params=pltpu.CompilerParams(
            dimension_semantics=("parallel",), vmem_limit_bytes=VMEM_LIMIT_BYTES),
        name="ffn",
    )(x2d, gain, wg, wu, wd)


def _mix_ffn_call(layer, x2d, ya, ybt, wo, gain, wg, wu, wd, fgain, final_norm):
    t = x2d.shape[0]
    tm = FFN_TOKENS
    tiles_per_seq = ybt.shape[2] // tm
    row_spec = pl.BlockSpec((tm, D_MODEL), lambda i: (i, 0))
    half_spec = pl.BlockSpec((tm, SGU_WIDTH), lambda i: (i, 0))
    ybt_spec = pl.BlockSpec((1, DIFF_WIDTH, tm),
                            lambda i: (i // tiles_per_seq, 0, i % tiles_per_seq))
    return pl.pallas_call(
        functools.partial(_mix_ffn_kernel, final_norm=final_norm),
        grid=(t // tm,),
        in_specs=[row_spec, half_spec, ybt_spec, _layer_spec(layer, (2 * SGU_WIDTH, D_MODEL)),
                  _layer_spec(layer, (1, D_MODEL)), _layer_spec(layer, (D_MODEL, D_FF)),
                  _layer_spec(layer, (D_MODEL, D_FF)), _layer_spec(layer, (D_FF, D_MODEL)),
                  _const_spec((1, D_MODEL))],
        out_specs=row_spec,
        out_shape=jax.ShapeDtypeStruct((t, D_MODEL), f32),
        compiler_params=pltpu.CompilerParams(
            dimension_semantics=("parallel",), vmem_limit_bytes=VMEM_LIMIT_BYTES),
        name="mix_ffn",
    )(x2d, ya, ybt, wo, gain, wg, wu, wd, fgain)


def _gelu_exact(x):
    return 0.5 * x * (1.0 + lax.erf(x * (1.0 / math.sqrt(2.0))))


def _inproj_kernel(x_ref, gain_ref, win_ref, wvt_ref, ones_ref, sgain_ref, ws_ref, bs_ref,
                   ya_ref, q_ref, k_ref, vt_ref):
    tm = x_ref.shape[0]
    h = _rmsnorm_rows(x_ref[...], gain_ref[...]).astype(bf16)

    qk = _dot(h, win_ref[:, 2 * SGU_WIDTH:2 * SGU_WIDTH + 2 * DIFF_WIDTH])
    q_ref[...] = (qk[:, :DIFF_WIDTH] * (DIFF_QK_DIM ** -0.5 * LOG2_E)).astype(bf16)
    k_ref[...] = qk[:, DIFF_WIDTH:].astype(bf16)

    vt = _dot_nt(wvt_ref[...], h).astype(bf16)
    ones_rows = jnp.ones((VT_ROWS - DIFF_V_DIM, ATTN_BLOCK), bf16)
    for hd in range(DIFF_HEADS):
        for c in range(tm // ATTN_BLOCK):
            vt_ref[0, hd, c, :DIFF_V_DIM, :] = vt[hd * DIFF_V_DIM:(hd + 1) * DIFF_V_DIM,
                                                  c * ATTN_BLOCK:(c + 1) * ATTN_BLOCK]
            vt_ref[0, hd, c, DIFF_V_DIM:, :] = ones_rows

    z = _gelu_exact(_dot(h, win_ref[:, :2 * SGU_WIDTH]))
    u = z[:, :SGU_WIDTH]
    v = z[:, SGU_WIDTH:]
    v2 = v * v
    v2_hi = v2.astype(bf16)
    v2_lo = (v2 - v2_hi.astype(f32)).astype(bf16)
    ones = ones_ref[...]
    ms = jnp.concatenate(
        [_dot(v2_hi[:, c:c + 256], ones) + _dot(v2_lo[:, c:c + 256], ones)
         for c in range(0, SGU_WIDTH, 256)], axis=-1) * (1.0 / SGU_GROUP_DIM)
    vn = (v * lax.rsqrt(ms + EPS) * sgain_ref[...]).astype(bf16)

    row = lax.broadcasted_iota(jnp.int32, (CHUNK, CHUNK), 0)
    col = lax.broadcasted_iota(jnp.int32, (CHUNK, CHUNK), 1)
    causal = col <= row
    for g in range(SGU_GROUPS):
        w = jnp.where(causal, ws_ref[g], 0.0).astype(bf16)
        bias = bs_ref[g]
        lo, hi = g * SGU_GROUP_DIM, (g + 1) * SGU_GROUP_DIM
        for c in range(tm // CHUNK):
            r0, r1 = c * CHUNK, (c + 1) * CHUNK
            gate = _dot(w, vn[r0:r1, lo:hi]) + bias
            ya_ref[r0:r1, lo:hi] = (u[r0:r1, lo:hi] * gate).astype(bf16)


def _inproj_call(layer, x2d, seq, gain, win, wvt, ones, sgain, ws, bs):
    t = x2d.shape[0]
    tm = INPROJ_TOKENS
    tiles_per_seq = seq // tm
    row_spec = pl.BlockSpec((tm, D_MODEL), lambda i: (i, 0))
    half_spec = pl.BlockSpec((tm, SGU_WIDTH), lambda i: (i, 0))
    half_shape = jax.ShapeDtypeStruct((t, SGU_WIDTH), bf16)
    vt_spec = pl.BlockSpec((1, DIFF_HEADS, tm // ATTN_BLOCK, VT_ROWS, ATTN_BLOCK),
                           lambda i: (i // tiles_per_seq, 0, i % tiles_per_seq, 0, 0))
    vt_shape = jax.ShapeDtypeStruct(
        (t // seq, DIFF_HEADS, seq // ATTN_BLOCK, VT_ROWS, ATTN_BLOCK), bf16)
    return pl.pallas_call(
        _inproj_kernel,
        grid=(t // tm,),
        in_specs=[row_spec, _layer_spec(layer, (1, D_MODEL)),
                  _layer_spec(layer, (D_MODEL, IN_COLS)),
                  _layer_spec(layer, (DIFF_WIDTH, D_MODEL)),
                  _const_spec((256, 256)), _layer_spec(layer, (1, SGU_WIDTH)),
                  _layer_spec(layer, (SGU_GROUPS, CHUNK, CHUNK)),
                  _layer_spec(layer, (SGU_GROUPS, CHUNK, 1))],
        out_specs=[half_spec, half_spec, half_spec, vt_spec],
        out_shape=[half_shape, half_shape, half_shape, vt_shape],
        compiler_params=pltpu.CompilerParams(
            dimension_semantics=("parallel",), vmem_limit_bytes=VMEM_LIMIT_BYTES),
        name="inproj_sgu",
    )(x2d, gain, win, wvt, ones, sgain, ws, bs)


def _attn_kernel(lq1_ref, lk1_ref, lq2_ref, lk2_ref, gain_ref, kbias_ref, q_ref, k_ref, vt_ref,
                 o_ref, qop_ref, m_ref, acc_ref, *, lam_init):
    blk = ATTN_BLOCK
    pair = pl.program_id(1)
    heads = range(DIFF_HEADS)
    slopes = [2.0 ** (-8.0 * (hd + 1) / DIFF_HEADS) * LOG2_E for hd in heads]
    krow = lax.broadcasted_iota(jnp.int32, (blk, 2 * blk), 0)
    qcol = lax.broadcasted_iota(jnp.int32, (blk, 2 * blk), 1) & (blk - 1)
    causal = krow <= qcol

    lane = lax.broadcasted_iota(jnp.int32, (blk, DIFF_V_DIM), 1)
    c1 = jnp.asarray(LOG2_E, f32).astype(bf16).astype(f32)
    q_tail = jnp.where(lane == 0, c1, jnp.where(lane == 1, LOG2_E - c1, 0.0)).astype(bf16)
    for qs in range(2):
        for hd in heads:
            q = q_ref[0, qs * blk:(qs + 1) * blk,
                      hd * DIFF_V_DIM:(hd + 1) * DIFF_V_DIM].astype(f32)
            qop_ref[qs, hd, :blk, :DIFF_V_DIM] = jnp.where(lane < DIFF_QK_DIM, q, 0.0).astype(bf16)
            qop_ref[qs, hd, blk:, :DIFF_V_DIM] = jnp.where(lane >= DIFF_QK_DIM, q, 0.0).astype(bf16)
            qop_ref[qs, hd, :blk, DIFF_V_DIM:] = q_tail
            qop_ref[qs, hd, blk:, DIFF_V_DIM:] = q_tail

    def scores(item):
        qs, kb, hd, _ = item
        k_rows = k_ref[0, pl.ds(pl.multiple_of(kb * blk, blk), blk),
                       hd * DIFF_V_DIM:(hd + 1) * DIFF_V_DIM]
        return _dot_nt(jnp.concatenate([k_rows, kbias_ref[hd]], axis=1), qop_ref[qs, hd])

    def softmax(item, s):
        qs, kb, hd, diagonal = item
        if diagonal:
            s = jnp.where(causal, s, NEG_INF)
            m_new = jnp.max(s, axis=0, keepdims=True)
            p, alpha = jnp.exp2(s - m_new), None
        else:
            beta = slopes[hd] * ((kb - (2 * pair + qs)) * blk).astype(f32)
            m_old = m_ref[qs, hd]
            m_new = jnp.maximum(m_old, jnp.max(s, axis=0, keepdims=True) + beta)
            p, alpha = jnp.exp2(s - (m_new - beta)), jnp.exp2(m_old - m_new)
        m_ref[qs, hd] = m_new
        return item, p.astype(bf16), alpha

    def accumulate(item, p, alpha):
        qs, kb, hd, _ = item
        pv = _dot(vt_ref[0, hd, kb], p)
        acc_ref[qs, hd] = pv if alpha is None else alpha * acc_ref[qs, hd] + pv

    def run(items):
        scored = [scores(item) for item in items[:ATTN_LOOKAHEAD]]
        weighted = []
        for n, item in enumerate(items):
            if n + ATTN_LOOKAHEAD < len(items):
                scored.append(scores(items[n + ATTN_LOOKAHEAD]))
            weighted.append(softmax(item, scored.pop(0)))
            if len(weighted) > ATTN_PV_DELAY:
                accumulate(*weighted.pop(0))
        for args in weighted:
            accumulate(*args)

    first, second = 2 * pair, 2 * pair + 1
    run([(0, first, hd, True) for hd in heads] + [(1, second, hd, True) for hd in heads]
        + [(1, first, hd, False) for hd in heads])

    def earlier_keys(j, carry):
        run([(qs, 2 * j + kbo, hd, False) for kbo in range(2) for qs in range(2) for hd in heads])
        return carry

    lax.fori_loop(0, pair, earlier_keys, 0)

    lam = (jnp.exp(jnp.sum(lq1_ref[...] * lk1_ref[...], axis=-1, keepdims=True))
           - jnp.exp(jnp.sum(lq2_ref[...] * lk2_ref[...], axis=-1, keepdims=True))
           + lam_init)
    gain = gain_ref[...] * (1.0 - lam_init)
    for qs in range(2):
        for hd in heads:
            acc1, acc2 = acc_ref[qs, hd, :, :blk], acc_ref[qs, hd, :, blk:]
            o = (acc1[:DIFF_V_DIM] / acc1[DIFF_V_DIM:DIFF_V_DIM + 1]
                 - lam * (acc2[:DIFF_V_DIM] / acc2[DIFF_V_DIM:DIFF_V_DIM + 1]))
            ms = jnp.mean(o * o, axis=0, keepdims=True)
            o_ref[0, hd * DIFF_V_DIM:(hd + 1) * DIFF_V_DIM, qs * blk:(qs + 1) * blk] = (
                o * lax.rsqrt(ms + EPS) * gain).astype(bf16)


def _attn_call(layer, lq1, lk1, lq2, lk2, gain, kbias, q, k, vt, lam_init):
    b, s, _ = q.shape
    blk = ATTN_BLOCK
    lam_spec = _layer_spec(layer, (1, DIFF_QK_DIM))
    q_spec = pl.BlockSpec((1, 2 * blk, DIFF_WIDTH), lambda bi, pi: (bi, pi, 0))
    k_spec = pl.BlockSpec((1, s, DIFF_WIDTH), lambda bi, pi: (bi, 0, 0))
    vt_spec = pl.BlockSpec((1, DIFF_HEADS, s // blk, VT_ROWS, blk),
                           lambda bi, pi: (bi, 0, 0, 0, 0))
    return pl.pallas_call(
        functools.partial(_attn_kernel, lam_init=lam_init),
        grid=(b, s // (2 * blk)),
        in_specs=[lam_spec, lam_spec, lam_spec, lam_spec, _layer_spec(layer, (DIFF_V_DIM, 1)),
                  _const_spec((DIFF_HEADS, blk, DIFF_V_DIM)), q_spec, k_spec, vt_spec],
        out_specs=pl.BlockSpec((1, DIFF_WIDTH, 2 * blk), lambda bi, pi: (bi, 0, pi)),
        out_shape=jax.ShapeDtypeStruct((b, DIFF_WIDTH, s), bf16),
        scratch_shapes=[pltpu.VMEM((2, DIFF_HEADS, 2 * blk, 2 * DIFF_V_DIM), bf16),
                        pltpu.VMEM((2, DIFF_HEADS, 1, 2 * blk), f32),
                        pltpu.VMEM((2, DIFF_HEADS, VT_ROWS, 2 * blk), f32)],
        compiler_params=pltpu.CompilerParams(
            dimension_semantics=("parallel", "arbitrary"),
            vmem_limit_bytes=VMEM_LIMIT_BYTES),
        name="diff_attn",
    )(lq1, lk1, lq2, lk2, gain, kbias, q, k, vt)


def _alibi_key_columns():
    r = jnp.arange(ATTN_BLOCK, dtype=f32)
    slopes = jnp.exp2(-8.0 * jnp.arange(1, DIFF_HEADS + 1, dtype=f32) / DIFF_HEADS)
    cols = jnp.zeros((DIFF_HEADS, ATTN_BLOCK, DIFF_V_DIM), f32)
    cols = cols.at[:, :, 0:2].set((slopes[:, None] * r[None, :])[:, :, None])
    return cols.astype(bf16)


def kernel(x, ffn1_norm, ffn1_w_gate, ffn1_w_up, ffn1_w_down, mix_norm, w_in, sgu_norm, sgu_w, sgu_b, lambda_q1, lambda_k1, lambda_q2, lambda_k2, diff_subln, w_out, ffn2_norm, ffn2_w_gate, ffn2_w_up, ffn2_w_down, final_norm):
    b, s, d = x.shape
    depth = w_in.shape[0]
    t = b * s
    x2d = x.reshape(t, d)
    group_id = jnp.arange(256) // SGU_GROUP_DIM
    ones = (group_id[:, None] == group_id[None, :]).astype(bf16)
    kbias = _alibi_key_columns()

    as_rows = lambda p: p.reshape(depth, 1, -1)
    wg1, wu1, wd1 = ffn1_w_gate.astype(bf16), ffn1_w_up.astype(bf16), ffn1_w_down.astype(bf16)
    wg2, wu2, wd2 = ffn2_w_gate.astype(bf16), ffn2_w_up.astype(bf16), ffn2_w_down.astype(bf16)
    win = w_in.astype(bf16)
    wvt = jnp.swapaxes(w_in[:, :, IN_COLS - DIFF_WIDTH:], 1, 2).astype(bf16)
    wo = w_out.astype(bf16)
    sgu_b_cols = sgu_b.reshape(depth, SGU_GROUPS, CHUNK, 1)
    subln_cols = diff_subln.reshape(depth, DIFF_V_DIM, 1)

    for l in range(depth):
        x2d = _ffn_call(l, x2d, as_rows(ffn1_norm), wg1, wu1, wd1)
        ya, q, k, vt = _inproj_call(l, x2d, s, as_rows(mix_norm), win, wvt, ones,
                                    as_rows(sgu_norm), sgu_w, sgu_b_cols)
        lam_init = 0.8 - 0.6 * math.exp(-0.3 * l)
        ybt = _attn_call(l, as_rows(lambda_q1), as_rows(lambda_k1), as_rows(lambda_q2),
                         as_rows(lambda_k2), subln_cols, kbias,
                         q.reshape(b, s, DIFF_WIDTH), k.reshape(b, s, DIFF_WIDTH), vt, lam_init)
        x2d = _mix_ffn_call(l, x2d, ya, ybt, wo, as_rows(ffn2_norm),
                            wg2, wu2, wd2, final_norm.reshape(1, d),
                            final_norm=(l == depth - 1))
    return x2d.reshape(b, s, d)
```

```python
import functools
import math

import jax
import jax.numpy as jnp
from jax import lax
from jax.experimental import pallas as pl
from jax.experimental.pallas import tpu as pltpu

D_MODEL = 1024
D_FF = 2816
SGU_WIDTH = 512
SGU_GROUP_DIM = 64
SGU_GROUPS = SGU_WIDTH // SGU_GROUP_DIM
CHUNK = 128
DIFF_WIDTH = 512
DIFF_V_DIM = 128
DIFF_QK_DIM = 64
DIFF_HEADS = DIFF_WIDTH // DIFF_V_DIM
IN_COLS = 2 * SGU_WIDTH + 3 * DIFF_WIDTH
EPS = 1e-6
NEG_INF = -1e30
LOG2_E = math.log2(math.e)

VMEM_LIMIT_BYTES = 56 * 1024 * 1024

FFN_TOKENS = 512
FFN_CHUNKS = ((0, 512), (512, 1024), (1024, 1536), (1536, 2048), (2048, 2560), (2560, 2816))
FFN_CHUNK_MAX = max(c1 - c0 for c0, c1 in FFN_CHUNKS)
assert FFN_CHUNK_MAX == SGU_WIDTH == DIFF_WIDTH
INPROJ_TOKENS = 512
ATTN_BLOCK = 256
VT_ROWS = DIFF_V_DIM + 16
ATTN_LOOKAHEAD = 3
ATTN_PV_DELAY = 2

bf16 = jnp.bfloat16
f32 = jnp.float32


def _rmsnorm_rows(x, gain):
    ms = jnp.mean(x * x, axis=-1, keepdims=True)
    return x * lax.rsqrt(ms + EPS) * gain


def _dot(a, b):
    return jnp.dot(a, b, preferred_element_type=f32)


def _dot_nt(a, b):
    return lax.dot_general(a, b, (((1,), (1,)), ((), ())), preferred_element_type=f32)


def _dot_tn(a, b):
    return lax.dot_general(a, b, (((0,), (0,)), ((), ())), preferred_element_type=f32)


class _FfnWeights:
    def __init__(self, layer, hbm, resident, stage_cols, stage_rows, sems):
        self.layer = layer
        self.wg_hbm, self.wu_hbm, self.wd_hbm = hbm
        self.wg, self.wu, self.wd = resident
        self.stage_cols, self.stage_rows, self.sems = stage_cols, stage_rows, sems

    def copies(self, c, kinds=(0, 1, 2)):
        c0, c1 = FFN_CHUNKS[c]
        w, slot = c1 - c0, c % 2
        endpoints = (
            (self.wg_hbm.at[self.layer, :, pl.ds(c0, w)], self.stage_cols.at[slot, 0, :, pl.ds(0, w)]),
            (self.wu_hbm.at[self.layer, :, pl.ds(c0, w)], self.stage_cols.at[slot, 1, :, pl.ds(0, w)]),
            (self.wd_hbm.at[self.layer, pl.ds(c0, w), :], self.stage_rows.at[slot, pl.ds(0, w), :]))
        return [pltpu.make_async_copy(*endpoints[k], self.sems.at[slot, k]) for k in kinds]

    def land(self, c):
        c0, c1 = FFN_CHUNKS[c]
        w, slot = c1 - c0, c % 2
        for copy in self.copies(c):
            copy.wait()
        self.wg[:, c0:c1] = self.stage_cols[slot, 0, :, :w].astype(bf16)
        self.wu[:, c0:c1] = self.stage_cols[slot, 1, :, :w].astype(bf16)
        self.wd[c0:c1, :] = self.stage_rows[slot, :w, :].astype(bf16)
        if c + 2 < len(FFN_CHUNKS):
            for copy in self.copies(c + 2):
                copy.start()


def _swiglu_half_step(x, gain, weights, landing):
    h = _rmsnorm_rows(x, gain).astype(bf16)
    acc = None
    for c, (c0, c1) in enumerate(FFN_CHUNKS):
        if landing:
            weights.land(c)
        g = _dot(h, weights.wg[:, c0:c1])
        u = _dot(h, weights.wu[:, c0:c1])
        a = (g / (1.0 + jnp.exp(-g)) * u).astype(bf16)
        d = _dot(a, weights.wd[c0:c1, :])
        acc = d if acc is None else acc + d
    return x + 0.5 * acc


def _ffn_kernel(x_ref, gain_ref, wg_hbm, wu_hbm, wd_hbm, o_ref,
                wg_ref, wu_ref, wd_ref, stage_cols, stage_rows, sems, *, layer):
    first_step = pl.program_id(0) == 0
    weights = _FfnWeights(layer, (wg_hbm, wu_hbm, wd_hbm), (wg_ref, wu_ref, wd_ref),
                          stage_cols, stage_rows, sems)

    @pl.when(first_step)
    def _():
        for copy in weights.copies(0) + weights.copies(1):
            copy.start()
        o_ref[...] = _swiglu_half_step(x_ref[...], gain_ref[...], weights, landing=True)

    @pl.when(jnp.logical_not(first_step))
    def _():
        o_ref[...] = _swiglu_half_step(x_ref[...], gain_ref[...], weights, landing=False)


def _mix_ffn_kernel(x_ref, ya_ref, ybt_ref, wo_hbm, gain_ref, wg_hbm, wu_hbm, wd_hbm, fgain_ref,
                    o_ref, wo_ref, wg_ref, wu_ref, wd_ref, stage_cols, stage_rows, sems,
                    *, layer, final_norm):
    first_step = pl.program_id(0) == 0
    weights = _FfnWeights(layer, (wg_hbm, wu_hbm, wd_hbm), (wg_ref, wu_ref, wd_ref),
                          stage_cols, stage_rows, sems)

    def stage_out_projection():
        halves = [pltpu.make_async_copy(wo_hbm.at[layer, pl.ds(s * SGU_WIDTH, SGU_WIDTH), :],
                                        stage_rows.at[s], sems.at[s, 2]) for s in range(2)]
        for copy in halves + weights.copies(0, (0, 1)) + weights.copies(1, (0, 1)):
            copy.start()
        for s, copy in enumerate(halves):
            copy.wait()
            wo_ref[s * SGU_WIDTH:(s + 1) * SGU_WIDTH, :] = stage_rows[s].astype(bf16)
        for copy in weights.copies(0, (2,)) + weights.copies(1, (2,)):
            copy.start()

    def step(landing):
        if landing:
            stage_out_projection()
        x = x_ref[...]
        x = (x + _dot(ya_ref[...], wo_ref[:SGU_WIDTH, :])
             + _dot_tn(ybt_ref[0], wo_ref[SGU_WIDTH:, :]))
        x = _swiglu_half_step(x, gain_ref[...], weights, landing)
        if final_norm:
            x = _rmsnorm_rows(x, fgain_ref[...])
        o_ref[...] = x

    pl.when(first_step)(functools.partial(step, True))
    pl.when(jnp.logical_not(first_step))(functools.partial(step, False))


def _const_spec(shape):
    nd = len(shape)
    return pl.BlockSpec(shape, lambda *_: (0,) * nd, pipeline_mode=pl.Buffered(1))


def _layer_spec(layer, shape):
    nd = len(shape)
    return pl.BlockSpec((None,) + tuple(shape), lambda *_: (layer,) + (0,) * nd,
                        pipeline_mode=pl.Buffered(1))


_HBM_SPEC = pl.BlockSpec(memory_space=pl.ANY)


def _ffn_weight_scratch():
    return [pltpu.VMEM((D_MODEL, D_FF), bf16), pltpu.VMEM((D_MODEL, D_FF), bf16),
            pltpu.VMEM((D_FF, D_MODEL), bf16),
            pltpu.VMEM((2, 2, D_MODEL, FFN_CHUNK_MAX), f32),
            pltpu.VMEM((2, FFN_CHUNK_MAX, D_MODEL), f32),
            pltpu.SemaphoreType.DMA((2, 3))]


def _ffn_call(layer, x2d, gain, wg, wu, wd):
    t = x2d.shape[0]
    tm = FFN_TOKENS
    row_spec = pl.BlockSpec((tm, D_MODEL), lambda i: (i, 0))
    return pl.pallas_call(
        functools.partial(_ffn_kernel, layer=layer),
        grid=(t // tm,),
        in_specs=[row_spec, _layer_spec(layer, (1, D_MODEL)), _HBM_SPEC, _HBM_SPEC, _HBM_SPEC],
        out_specs=row_spec,
        out_shape=jax.ShapeDtypeStruct((t, D_MODEL), f32),
        scratch_shapes=_ffn_weight_scratch(),
        compiler_params=pltpu.CompilerParams(
            dimension_semantics=("arbitrary",), vmem_limit_bytes=VMEM_LIMIT_BYTES),
        name="ffn",
    )(x2d, gain, wg, wu, wd)


def _mix_ffn_call(layer, x2d, ya, ybt, wo, gain, wg, wu, wd, fgain, final_norm):
    t = x2d.shape[0]
    tm = FFN_TOKENS
    tiles_per_seq = ybt.shape[2] // tm
    row_spec = pl.BlockSpec((tm, D_MODEL), lambda i: (i, 0))
    half_spec = pl.BlockSpec((tm, SGU_WIDTH), lambda i: (i, 0))
    ybt_spec = pl.BlockSpec((1, DIFF_WIDTH, tm),
                            lambda i: (i // tiles_per_seq, 0, i % tiles_per_seq))
    return pl.pallas_call(
        functools.partial(_mix_ffn_kernel, layer=layer, final_norm=final_norm),
        grid=(t // tm,),
        in_specs=[row_spec, half_spec, ybt_spec, _HBM_SPEC, _layer_spec(layer, (1, D_MODEL)),
                  _HBM_SPEC, _HBM_SPEC, _HBM_SPEC, _const_spec((1, D_MODEL))],
        out_specs=row_spec,
        out_shape=jax.ShapeDtypeStruct((t, D_MODEL), f32),
        scratch_shapes=[pltpu.VMEM((2 * SGU_WIDTH, D_MODEL), bf16)] + _ffn_weight_scratch(),
        compiler_params=pltpu.CompilerParams(
            dimension_semantics=("arbitrary",), vmem_limit_bytes=VMEM_LIMIT_BYTES),
        name="mix_ffn",
    )(x2d, ya, ybt, wo, gain, wg, wu, wd, fgain)


def _gelu_exact(x):
    return 0.5 * x * (1.0 + lax.erf(x * (1.0 / math.sqrt(2.0))))


_COL_ZV, _COL_ZU, _COL_Q, _COL_K, _COL_V = (
    SGU_WIDTH, 0, 2 * SGU_WIDTH, 2 * SGU_WIDTH + DIFF_WIDTH, 2 * SGU_WIDTH + 2 * DIFF_WIDTH)
INPROJ_COL_ORDER = (_COL_ZV, _COL_ZU, _COL_Q, _COL_K, _COL_V)
INPROJ_COL_WIDTH = 512
assert SGU_WIDTH == DIFF_WIDTH == INPROJ_COL_WIDTH


def _inproj_kernel(x_ref, gain_ref, win_hbm, ones_ref, sgain_ref, ws_ref, bs_ref,
                   ya_ref, q_ref, k_ref, vt_ref, win_ref, stage_ref, sems, *, layer):
    tm = x_ref.shape[0]
    width = INPROJ_COL_WIDTH
    first_step = pl.program_id(0) == 0

    def copy(n):
        return pltpu.make_async_copy(
            win_hbm.at[layer, :, pl.ds(INPROJ_COL_ORDER[n], width)],
            stage_ref.at[n % 2], sems.at[n % 2])

    def step(landing):
        def columns(n):
            c0 = INPROJ_COL_ORDER[n]
            if landing:
                copy(n).wait()
                win_ref[:, c0:c0 + width] = stage_ref[n % 2].astype(bf16)
                if n + 2 < len(INPROJ_COL_ORDER):
                    copy(n + 2).start()
            return win_ref[:, c0:c0 + width]

        if landing:
            copy(0).start()
            copy(1).start()
        h = _rmsnorm_rows(x_ref[...], gain_ref[...]).astype(bf16)

        v = _gelu_exact(_dot(h, columns(0)))
        zu = _dot(h, columns(1))
        v2 = v * v
        v2_hi = v2.astype(bf16)
        v2_lo = (v2 - v2_hi.astype(f32)).astype(bf16)
        ones = ones_ref[...]
        ms = jnp.concatenate(
            [_dot(v2_hi[:, c:c + 256], ones) + _dot(v2_lo[:, c:c + 256], ones)
             for c in range(0, SGU_WIDTH, 256)], axis=-1) * (1.0 / SGU_GROUP_DIM)
        q = _dot(h, columns(2))
        u = _gelu_exact(zu)
        vn = (v * lax.rsqrt(ms + EPS) * sgain_ref[...]).astype(bf16)
        k = _dot(h, columns(3))
        q_ref[...] = (q * (DIFF_QK_DIM ** -0.5 * LOG2_E)).astype(bf16)

        row = lax.broadcasted_iota(jnp.int32, (CHUNK, CHUNK), 0)
        col = lax.broadcasted_iota(jnp.int32, (CHUNK, CHUNK), 1)
        causal = col <= row
        gates = []
        for g in range(SGU_GROUPS):
            w = jnp.where(causal, ws_ref[g], 0.0).astype(bf16)
            lo, hi = g * SGU_GROUP_DIM, (g + 1) * SGU_GROUP_DIM
            gates.append([_dot(w, vn[c * CHUNK:(c + 1) * CHUNK, lo:hi]) + bs_ref[g]
                          for c in range(tm // CHUNK)])
        k_ref[...] = k.astype(bf16)

        vt = _dot_nt_t(columns(4), h)
        for g in range(SGU_GROUPS):
            lo, hi = g * SGU_GROUP_DIM, (g + 1) * SGU_GROUP_DIM
            for c in range(tm // CHUNK):
                r0, r1 = c * CHUNK, (c + 1) * CHUNK
                ya_ref[r0:r1, lo:hi] = (u[r0:r1, lo:hi] * gates[g][c]).astype(bf16)

        vt = vt.astype(bf16)
        ones_rows = jnp.ones((VT_ROWS - DIFF_V_DIM, ATTN_BLOCK), bf16)
        for hd in range(DIFF_HEADS):
            for c in range(tm // ATTN_BLOCK):
                vt_ref[0, hd, c, :DIFF_V_DIM, :] = vt[hd * DIFF_V_DIM:(hd + 1) * DIFF_V_DIM,
                                                      c * ATTN_BLOCK:(c + 1) * ATTN_BLOCK]
                vt_ref[0, hd, c, DIFF_V_DIM:, :] = ones_rows

    pl.when(first_step)(functools.partial(step, True))
    pl.when(jnp.logical_not(first_step))(functools.partial(step, False))


def _dot_nt_t(w, h):
    return lax.dot_general(w, h, (((0,), (1,)), ((), ())), preferred_element_type=f32)


def _inproj_call(layer, x2d, seq, gain, win, ones, sgain, ws, bs):
    t = x2d.shape[0]
    tm = INPROJ_TOKENS
    tiles_per_seq = seq // tm
    row_spec = pl.BlockSpec((tm, D_MODEL), lambda i: (i, 0))
    half_spec = pl.BlockSpec((tm, SGU_WIDTH), lambda i: (i, 0))
    half_shape = jax.ShapeDtypeStruct((t, SGU_WIDTH), bf16)
    vt_spec = pl.BlockSpec((1, DIFF_HEADS, tm // ATTN_BLOCK, VT_ROWS, ATTN_BLOCK),
                           lambda i: (i // tiles_per_seq, 0, i % tiles_per_seq, 0, 0))
    vt_shape = jax.ShapeDtypeStruct(
        (t // seq, DIFF_HEADS, seq // ATTN_BLOCK, VT_ROWS, ATTN_BLOCK), bf16)
    return pl.pallas_call(
        functools.partial(_inproj_kernel, layer=layer),
        grid=(t // tm,),
        in_specs=[row_spec, _layer_spec(layer, (1, D_MODEL)), _HBM_SPEC,
                  _const_spec((256, 256)), _layer_spec(layer, (1, SGU_WIDTH)),
                  _layer_spec(layer, (SGU_GROUPS, CHUNK, CHUNK)),
                  _layer_spec(layer, (SGU_GROUPS, CHUNK, 1))],
        out_specs=[half_spec, half_spec, half_spec, vt_spec],
        out_shape=[half_shape, half_shape, half_shape, vt_shape],
        scratch_shapes=[pltpu.VMEM((D_MODEL, IN_COLS), bf16),
                        pltpu.VMEM((2, D_MODEL, INPROJ_COL_WIDTH), f32),
                        pltpu.SemaphoreType.DMA((2,))],
        compiler_params=pltpu.CompilerParams(
            dimension_semantics=("arbitrary",), vmem_limit_bytes=VMEM_LIMIT_BYTES),
        name="inproj_sgu",
    )(x2d, gain, win, ones, sgain, ws, bs)


def _attn_kernel(lq1_ref, lk1_ref, lq2_ref, lk2_ref, gain_ref, kbias_ref, q_ref, k_ref, vt_ref,
                 o_ref, qop_ref, m_ref, acc_ref, *, lam_init):
    blk = ATTN_BLOCK
    pair = pl.program_id(1)
    heads = range(DIFF_HEADS)
    slopes = [2.0 ** (-8.0 * (hd + 1) / DIFF_HEADS) * LOG2_E for hd in heads]
    krow = lax.broadcasted_iota(jnp.int32, (blk, 2 * blk), 0)
    qcol = lax.broadcasted_iota(jnp.int32, (blk, 2 * blk), 1) & (blk - 1)
    causal = krow <= qcol

    lane = lax.broadcasted_iota(jnp.int32, (blk, DIFF_V_DIM), 1)
    c1 = jnp.asarray(LOG2_E, f32).astype(bf16).astype(f32)
    q_tail = jnp.where(lane == 0, c1, jnp.where(lane == 1, LOG2_E - c1, 0.0)).astype(bf16)
    for qs in range(2):
        for hd in heads:
            q = q_ref[0, qs * blk:(qs + 1) * blk,
                      hd * DIFF_V_DIM:(hd + 1) * DIFF_V_DIM].astype(f32)
            qop_ref[qs, hd, :blk, :DIFF_V_DIM] = jnp.where(lane < DIFF_QK_DIM, q, 0.0).astype(bf16)
            qop_ref[qs, hd, blk:, :DIFF_V_DIM] = jnp.where(lane >= DIFF_QK_DIM, q, 0.0).astype(bf16)
            qop_ref[qs, hd, :blk, DIFF_V_DIM:] = q_tail
            qop_ref[qs, hd, blk:, DIFF_V_DIM:] = q_tail

    def scores(item):
        qs, kb, hd, _ = item
        k_rows = k_ref[0, pl.ds(pl.multiple_of(kb * blk, blk), blk),
                       hd * DIFF_V_DIM:(hd + 1) * DIFF_V_DIM]
        return _dot_nt(jnp.concatenate([k_rows, kbias_ref[hd]], axis=1), qop_ref[qs, hd])

    def softmax(item, s):
        qs, kb, hd, diagonal = item
        if diagonal:
            s = jnp.where(causal, s, NEG_INF)
            m_new = jnp.max(s, axis=0, keepdims=True)
            p, alpha = jnp.exp2(s - m_new), None
        else:
            beta = slopes[hd] * ((kb - (2 * pair + qs)) * blk).astype(f32)
            m_old = m_ref[qs, hd]
            m_new = jnp.maximum(m_old, jnp.max(s, axis=0, keepdims=True) + beta)
            p, alpha = jnp.exp2(s - (m_new - beta)), jnp.exp2(m_old - m_new)
        m_ref[qs, hd] = m_new
        return item, p.astype(bf16), alpha

    def accumulate(item, p, alpha):
        qs, kb, hd, _ = item
        pv = _dot(vt_ref[0, hd, kb], p)
        acc_ref[qs, hd] = pv if alpha is None else alpha * acc_ref[qs, hd] + pv

    def run(items):
        scored = [scores(item) for item in items[:ATTN_LOOKAHEAD]]
        weighted = []
        for n, item in enumerate(items):
            if n + ATTN_LOOKAHEAD < len(items):
                scored.append(scores(items[n + ATTN_LOOKAHEAD]))
            weighted.append(softmax(item, scored.pop(0)))
            if len(weighted) > ATTN_PV_DELAY:
                accumulate(*weighted.pop(0))
        for args in weighted:
            accumulate(*args)

    first, second = 2 * pair, 2 * pair + 1
    run([(0, first, hd, True) for hd in heads] + [(1, second, hd, True) for hd in heads]
        + [(1, first, hd, False) for hd in heads])

    def earlier_keys(j, carry):
        run([(qs, 2 * j + kbo, hd, False) for kbo in range(2) for qs in range(2) for hd in heads])
        return carry

    lax.fori_loop(0, pair, earlier_keys, 0)

    lam = (jnp.exp(jnp.sum(lq1_ref[...] * lk1_ref[...], axis=-1, keepdims=True))
           - jnp.exp(jnp.sum(lq2_ref[...] * lk2_ref[...], axis=-1, keepdims=True))
           + lam_init)
    gain = gain_ref[...] * (1.0 - lam_init)
    for qs in range(2):
        for hd in heads:
            acc1, acc2 = acc_ref[qs, hd, :, :blk], acc_ref[qs, hd, :, blk:]
            o = (acc1[:DIFF_V_DIM] / acc1[DIFF_V_DIM:DIFF_V_DIM + 1]
                 - lam * (acc2[:DIFF_V_DIM] / acc2[DIFF_V_DIM:DIFF_V_DIM + 1]))
            ms = jnp.mean(o * o, axis=0, keepdims=True)
            o_ref[0, hd * DIFF_V_DIM:(hd + 1) * DIFF_V_DIM, qs * blk:(qs + 1) * blk] = (
                o * lax.rsqrt(ms + EPS) * gain).astype(bf16)


def _attn_call(layer, lq1, lk1, lq2, lk2, gain, kbias, q, k, vt, lam_init):
    b, s, _ = q.shape
    blk = ATTN_BLOCK
    lam_spec = _layer_spec(layer, (1, DIFF_QK_DIM))
    q_spec = pl.BlockSpec((1, 2 * blk, DIFF_WIDTH), lambda bi, pi: (bi, pi, 0))
    k_spec = pl.BlockSpec((1, s, DIFF_WIDTH), lambda bi, pi: (bi, 0, 0))
    vt_spec = pl.BlockSpec((1, DIFF_HEADS, s // blk, VT_ROWS, blk),
                           lambda bi, pi: (bi, 0, 0, 0, 0))
    return pl.pallas_call(
        functools.partial(_attn_kernel, lam_init=lam_init),
        grid=(b, s // (2 * blk)),
        in_specs=[lam_spec, lam_spec, lam_spec, lam_spec, _layer_spec(layer, (DIFF_V_DIM, 1)),
                  _const_spec((DIFF_HEADS, blk, DIFF_V_DIM)), q_spec, k_spec, vt_spec],
        out_specs=pl.BlockSpec((1, DIFF_WIDTH, 2 * blk), lambda bi, pi: (bi, 0, pi)),
        out_shape=jax.ShapeDtypeStruct((b, DIFF_WIDTH, s), bf16),
        scratch_shapes=[pltpu.VMEM((2, DIFF_HEADS, 2 * blk, 2 * DIFF_V_DIM), bf16),
                        pltpu.VMEM((2, DIFF_HEADS, 1, 2 * blk), f32),
                        pltpu.VMEM((2, DIFF_HEADS, VT_ROWS, 2 * blk), f32)],
        compiler_params=pltpu.CompilerParams(
            dimension_semantics=("parallel", "arbitrary"),
            vmem_limit_bytes=VMEM_LIMIT_BYTES),
        name="diff_attn",
    )(lq1, lk1, lq2, lk2, gain, kbias, q, k, vt)


def _alibi_key_columns():
    r = jnp.arange(ATTN_BLOCK, dtype=f32)
    slopes = jnp.exp2(-8.0 * jnp.arange(1, DIFF_HEADS + 1, dtype=f32) / DIFF_HEADS)
    cols = jnp.zeros((DIFF_HEADS, ATTN_BLOCK, DIFF_V_DIM), f32)
    cols = cols.at[:, :, 0:2].set((slopes[:, None] * r[None, :])[:, :, None])
    return cols.astype(bf16)


def kernel(x, ffn1_norm, ffn1_w_gate, ffn1_w_up, ffn1_w_down, mix_norm, w_in, sgu_norm, sgu_w, sgu_b, lambda_q1, lambda_k1, lambda_q2, lambda_k2, diff_subln, w_out, ffn2_norm, ffn2_w_gate, ffn2_w_up, ffn2_w_down, final_norm):
    b, s, d = x.shape
    depth = w_in.shape[0]
    t = b * s
    x2d = x.reshape(t, d)
    group_id = jnp.arange(256) // SGU_GROUP_DIM
    ones = (group_id[:, None] == group_id[None, :]).astype(bf16)
    kbias = _alibi_key_columns()

    as_rows = lambda p: p.reshape(depth, 1, -1)
    wg1, wu1, wd1 = ffn1_w_gate, ffn1_w_up, ffn1_w_down
    wg2, wu2, wd2 = ffn2_w_gate, ffn2_w_up, ffn2_w_down
    win = w_in
    wo = w_out
    sgu_b_cols = sgu_b.reshape(depth, SGU_GROUPS, CHUNK, 1)
    subln_cols = diff_subln.reshape(depth, DIFF_V_DIM, 1)

    for l in range(depth):
        x2d = _ffn_call(l, x2d, as_rows(ffn1_norm), wg1, wu1, wd1)
        ya, q, k, vt = _inproj_call(l, x2d, s, as_rows(mix_norm), win, ones,
                                    as_rows(sgu_norm), sgu_w, sgu_b_cols)
        lam_init = 0.8 - 0.6 * math.exp(-0.3 * l)
        ybt = _attn_call(l, as_rows(lambda_q1), as_rows(lambda_k1), as_rows(lambda_q2),
                         as_rows(lambda_k2), subln_cols, kbias,
                         q.reshape(b, s, DIFF_WIDTH), k.reshape(b, s, DIFF_WIDTH), vt, lam_init)
        x2d = _mix_ffn_call(l, x2d, ya, ybt, wo, as_rows(ffn2_norm),
                            wg2, wu2, wd2, final_norm.reshape(1, d),
                            final_norm=(l == depth - 1))
    return x2d.reshape(b, s, d)
```

```python
import functools
import math

import jax
import jax.numpy as jnp
from jax import lax
from jax.experimental import pallas as pl
from jax.experimental.pallas import tpu as pltpu

D_MODEL = 1024
D_FF = 2816
SGU_WIDTH = 512
SGU_GROUP_DIM = 64
SGU_GROUPS = SGU_WIDTH // SGU_GROUP_DIM
CHUNK = 128
DIFF_WIDTH = 512
DIFF_V_DIM = 128
DIFF_QK_DIM = 64
DIFF_HEADS = DIFF_WIDTH // DIFF_V_DIM
IN_COLS = 2 * SGU_WIDTH + 3 * DIFF_WIDTH
EPS = 1e-6
NEG_INF = -1e30
LOG2_E = math.log2(math.e)

VMEM_LIMIT_BYTES = 56 * 1024 * 1024

FFN_TOKENS = 512
FFN_CHUNKS = ((0, 512), (512, 1024), (1024, 1536), (1536, 2048), (2048, 2560), (2560, 2816))
FFN_CHUNK_MAX = max(c1 - c0 for c0, c1 in FFN_CHUNKS)
assert FFN_CHUNK_MAX == SGU_WIDTH == DIFF_WIDTH
INPROJ_TOKENS = 512
ATTN_BLOCK = 256
VT_ROWS = DIFF_V_DIM + 16
ATTN_LOOKAHEAD = 3
ATTN_PV_DELAY = 2

bf16 = jnp.bfloat16
f32 = jnp.float32


def _rmsnorm_rows(x, gain):
    ms = jnp.mean(x * x, axis=-1, keepdims=True)
    return x * lax.rsqrt(ms + EPS) * gain


def _dot(a, b):
    return jnp.dot(a, b, preferred_element_type=f32)


def _dot_nt(a, b):
    return lax.dot_general(a, b, (((1,), (1,)), ((), ())), preferred_element_type=f32)


def _dot_tn(a, b):
    return lax.dot_general(a, b, (((0,), (0,)), ((), ())), preferred_element_type=f32)


class _FfnWeights:
    def __init__(self, layer, hbm, resident, stage_cols, stage_rows, sems):
        self.layer = layer
        self.wg_hbm, self.wu_hbm, self.wd_hbm = hbm
        self.wg, self.wu, self.wd = resident
        self.stage_cols, self.stage_rows, self.sems = stage_cols, stage_rows, sems

    def copies(self, c, kinds=(0, 1, 2)):
        c0, c1 = FFN_CHUNKS[c]
        w, slot = c1 - c0, c % 2
        endpoints = (
            (self.wg_hbm.at[self.layer, :, pl.ds(c0, w)], self.stage_cols.at[slot, 0, :, pl.ds(0, w)]),
            (self.wu_hbm.at[self.layer, :, pl.ds(c0, w)], self.stage_cols.at[slot, 1, :, pl.ds(0, w)]),
            (self.wd_hbm.at[self.layer, pl.ds(c0, w), :], self.stage_rows.at[slot, pl.ds(0, w), :]))
        return [pltpu.make_async_copy(*endpoints[k], self.sems.at[slot, k]) for k in kinds]

    def land(self, c):
        c0, c1 = FFN_CHUNKS[c]
        w, slot = c1 - c0, c % 2
        for copy in self.copies(c):
            copy.wait()
        self.wg[:, c0:c1] = self.stage_cols[slot, 0, :, :w].astype(bf16)
        self.wu[:, c0:c1] = self.stage_cols[slot, 1, :, :w].astype(bf16)
        self.wd[c0:c1, :] = self.stage_rows[slot, :w, :].astype(bf16)
        if c + 2 < len(FFN_CHUNKS):
            for copy in self.copies(c + 2):
                copy.start()


def _swiglu_half_step(x, gain, weights, landing):
    h = _rmsnorm_rows(x, gain).astype(bf16)
    acc = None
    for c, (c0, c1) in enumerate(FFN_CHUNKS):
        if landing:
            weights.land(c)
        g = _dot(h, weights.wg[:, c0:c1])
        u = _dot(h, weights.wu[:, c0:c1])
        a = (g / (1.0 + jnp.exp(-g)) * u).astype(bf16)
        d = _dot(a, weights.wd[c0:c1, :])
        acc = d if acc is None else acc + d
    return x + 0.5 * acc


def _ffn_kernel(x_ref, gain_ref, wg_hbm, wu_hbm, wd_hbm, o_ref,
                wg_ref, wu_ref, wd_ref, stage_cols, stage_rows, sems, *, layer):
    first_step = pl.program_id(0) == 0
    weights = _FfnWeights(layer, (wg_hbm, wu_hbm, wd_hbm), (wg_ref, wu_ref, wd_ref),
                          stage_cols, stage_rows, sems)

    @pl.when(first_step)
    def _():
        for copy in weights.copies(0) + weights.copies(1):
            copy.start()
        o_ref[...] = _swiglu_half_step(x_ref[...], gain_ref[...], weights, landing=True)

    @pl.when(jnp.logical_not(first_step))
    def _():
        o_ref[...] = _swiglu_half_step(x_ref[...], gain_ref[...], weights, landing=False)


def _mix_ffn_kernel(x_ref, ya_ref, ybt_ref, wo_hbm, gain_ref, wg_hbm, wu_hbm, wd_hbm, fgain_ref,
                    o_ref, wo_ref, wg_ref, wu_ref, wd_ref, stage_cols, stage_rows, sems,
                    *, layer, final_norm):
    first_step = pl.program_id(0) == 0
    weights = _FfnWeights(layer, (wg_hbm, wu_hbm, wd_hbm), (wg_ref, wu_ref, wd_ref),
                          stage_cols, stage_rows, sems)

    def stage_out_projection():
        halves = [pltpu.make_async_copy(wo_hbm.at[layer, pl.ds(s * SGU_WIDTH, SGU_WIDTH), :],
                                        stage_rows.at[s], sems.at[s, 2]) for s in range(2)]
        for copy in halves + weights.copies(0, (0, 1)) + weights.copies(1, (0, 1)):
            copy.start()
        for s, copy in enumerate(halves):
            copy.wait()
            wo_ref[s * SGU_WIDTH:(s + 1) * SGU_WIDTH, :] = stage_rows[s].astype(bf16)
        for copy in weights.copies(0, (2,)) + weights.copies(1, (2,)):
            copy.start()

    def step(landing):
        if landing:
            stage_out_projection()
        x = x_ref[...]
        x = (x + _dot(ya_ref[...], wo_ref[:SGU_WIDTH, :])
             + _dot_tn(ybt_ref[0], wo_ref[SGU_WIDTH:, :]))
        x = _swiglu_half_step(x, gain_ref[...], weights, landing)
        if final_norm:
            x = _rmsnorm_rows(x, fgain_ref[...])
        o_ref[...] = x

    pl.when(first_step)(functools.partial(step, True))
    pl.when(jnp.logical_not(first_step))(functools.partial(step, False))


def _const_spec(shape):
    nd = len(shape)
    return pl.BlockSpec(shape, lambda *_: (0,) * nd, pipeline_mode=pl.Buffered(1))


def _layer_spec(layer, shape):
    nd = len(shape)
    return pl.BlockSpec((None,) + tuple(shape), lambda *_: (layer,) + (0,) * nd,
                        pipeline_mode=pl.Buffered(1))


_HBM_SPEC = pl.BlockSpec(memory_space=pl.ANY)


def _ffn_weight_scratch():
    return [pltpu.VMEM((D_MODEL, D_FF), bf16), pltpu.VMEM((D_MODEL, D_FF), bf16),
            pltpu.VMEM((D_FF, D_MODEL), bf16),
            pltpu.VMEM((2, 2, D_MODEL, FFN_CHUNK_MAX), f32),
            pltpu.VMEM((2, FFN_CHUNK_MAX, D_MODEL), f32),
            pltpu.SemaphoreType.DMA((2, 3))]


def _ffn_call(layer, x2d, gain, wg, wu, wd):
    t = x2d.shape[0]
    tm = FFN_TOKENS
    row_spec = pl.BlockSpec((tm, D_MODEL), lambda i: (i, 0))
    return pl.pallas_call(
        functools.partial(_ffn_kernel, layer=layer),
        grid=(t // tm,),
        in_specs=[row_spec, _layer_spec(layer, (1, D_MODEL)), _HBM_SPEC, _HBM_SPEC, _HBM_SPEC],
        out_specs=row_spec,
        out_shape=jax.ShapeDtypeStruct((t, D_MODEL), f32),
        scratch_shapes=_ffn_weight_scratch(),
        compiler_params=pltpu.CompilerParams(
            dimension_semantics=("arbitrary",), vmem_limit_bytes=VMEM_LIMIT_BYTES),
        name="ffn",
    )(x2d, gain, wg, wu, wd)


def _mix_ffn_call(layer, x2d, ya, ybt, wo, gain, wg, wu, wd, fgain, final_norm):
    t = x2d.shape[0]
    tm = FFN_TOKENS
    tiles_per_seq = ybt.shape[2] // tm
    row_spec = pl.BlockSpec((tm, D_MODEL), lambda i: (i, 0))
    half_spec = pl.BlockSpec((tm, SGU_WIDTH), lambda i: (i, 0))
    ybt_spec = pl.BlockSpec((1, DIFF_WIDTH, tm),
                            lambda i: (i // tiles_per_seq, 0, i % tiles_per_seq))
    return pl.pallas_call(
        functools.partial(_mix_ffn_kernel, layer=layer, final_norm=final_norm),
        grid=(t // tm,),
        in_specs=[row_spec, half_spec, ybt_spec, _HBM_SPEC, _layer_spec(layer, (1, D_MODEL)),
                  _HBM_SPEC, _HBM_SPEC, _HBM_SPEC, _const_spec((1, D_MODEL))],
        out_specs=row_spec,
        out_shape=jax.ShapeDtypeStruct((t, D_MODEL), f32),
        scratch_shapes=[pltpu.VMEM((2 * SGU_WIDTH, D_MODEL), bf16)] + _ffn_weight_scratch(),
        compiler_params=pltpu.CompilerParams(
            dimension_semantics=("arbitrary",), vmem_limit_bytes=VMEM_LIMIT_BYTES),
        name="mix_ffn",
    )(x2d, ya, ybt, wo, gain, wg, wu, wd, fgain)


def _gelu_exact(x):
    return 0.5 * x * (1.0 + lax.erf(x * (1.0 / math.sqrt(2.0))))


_COL_ZV, _COL_ZU, _COL_Q, _COL_K, _COL_V = (
    SGU_WIDTH, 0, 2 * SGU_WIDTH, 2 * SGU_WIDTH + DIFF_WIDTH, 2 * SGU_WIDTH + 2 * DIFF_WIDTH)
INPROJ_COL_ORDER = (_COL_ZV, _COL_ZU, _COL_Q, _COL_K, _COL_V)
INPROJ_COL_WIDTH = 512
assert SGU_WIDTH == DIFF_WIDTH == INPROJ_COL_WIDTH


def _inproj_kernel(x_ref, gain_ref, win_hbm, ones_ref, sgain_ref, ws_ref, bs_ref,
                   ya_ref, qt_ref, k_ref, vt_ref, win_ref, stage_ref, sems, *, layer):
    tm = x_ref.shape[0]
    width = INPROJ_COL_WIDTH
    first_step = pl.program_id(0) == 0

    def copy(n):
        return pltpu.make_async_copy(
            win_hbm.at[layer, :, pl.ds(INPROJ_COL_ORDER[n], width)],
            stage_ref.at[n % 2], sems.at[n % 2])

    def step(landing):
        def columns(n):
            c0 = INPROJ_COL_ORDER[n]
            if landing:
                copy(n).wait()
                win_ref[:, c0:c0 + width] = stage_ref[n % 2].astype(bf16)
                if n + 2 < len(INPROJ_COL_ORDER):
                    copy(n + 2).start()
            return win_ref[:, c0:c0 + width]

        if landing:
            copy(0).start()
            copy(1).start()
        h = _rmsnorm_rows(x_ref[...], gain_ref[...]).astype(bf16)

        v = _gelu_exact(_dot(h, columns(0)))
        zu = _dot(h, columns(1))
        v2 = v * v
        v2_hi = v2.astype(bf16)
        v2_lo = (v2 - v2_hi.astype(f32)).astype(bf16)
        ones = ones_ref[...]
        ms = jnp.concatenate(
            [_dot(v2_hi[:, c:c + 256], ones) + _dot(v2_lo[:, c:c + 256], ones)
             for c in range(0, SGU_WIDTH, 256)], axis=-1) * (1.0 / SGU_GROUP_DIM)
        qt = _dot_nt_t(columns(2), h)
        u = _gelu_exact(zu)
        vn = (v * lax.rsqrt(ms + EPS) * sgain_ref[...]).astype(bf16)
        k = _dot(h, columns(3))
        qt_ref[0] = (qt * (DIFF_QK_DIM ** -0.5 * LOG2_E)).astype(bf16)

        row = lax.broadcasted_iota(jnp.int32, (CHUNK, CHUNK), 0)
        col = lax.broadcasted_iota(jnp.int32, (CHUNK, CHUNK), 1)
        causal = col <= row
        gates = []
        for g in range(SGU_GROUPS):
            w = jnp.where(causal, ws_ref[g], 0.0).astype(bf16)
            lo, hi = g * SGU_GROUP_DIM, (g + 1) * SGU_GROUP_DIM
            gates.append([_dot(w, vn[c * CHUNK:(c + 1) * CHUNK, lo:hi]) + bs_ref[g]
                          for c in range(tm // CHUNK)])
        k_ref[...] = k.astype(bf16)

        vt = _dot_nt_t(columns(4), h)
        for g in range(SGU_GROUPS):
            lo, hi = g * SGU_GROUP_DIM, (g + 1) * SGU_GROUP_DIM
            for c in range(tm // CHUNK):
                r0, r1 = c * CHUNK, (c + 1) * CHUNK
                ya_ref[r0:r1, lo:hi] = (u[r0:r1, lo:hi] * gates[g][c]).astype(bf16)

        vt = vt.astype(bf16)
        ones_rows = jnp.ones((VT_ROWS - DIFF_V_DIM, ATTN_BLOCK), bf16)
        for hd in range(DIFF_HEADS):
            for c in range(tm // ATTN_BLOCK):
                vt_ref[0, hd, c, :DIFF_V_DIM, :] = vt[hd * DIFF_V_DIM:(hd + 1) * DIFF_V_DIM,
                                                      c * ATTN_BLOCK:(c + 1) * ATTN_BLOCK]
                vt_ref[0, hd, c, DIFF_V_DIM:, :] = ones_rows

    pl.when(first_step)(functools.partial(step, True))
    pl.when(jnp.logical_not(first_step))(functools.partial(step, False))


def _dot_nt_t(w, h):
    return lax.dot_general(w, h, (((0,), (1,)), ((), ())), preferred_element_type=f32)


def _inproj_call(layer, x2d, seq, gain, win, ones, sgain, ws, bs):
    t = x2d.shape[0]
    tm = INPROJ_TOKENS
    tiles_per_seq = seq // tm
    row_spec = pl.BlockSpec((tm, D_MODEL), lambda i: (i, 0))
    half_spec = pl.BlockSpec((tm, SGU_WIDTH), lambda i: (i, 0))
    half_shape = jax.ShapeDtypeStruct((t, SGU_WIDTH), bf16)
    vt_spec = pl.BlockSpec((1, DIFF_HEADS, tm // ATTN_BLOCK, VT_ROWS, ATTN_BLOCK),
                           lambda i: (i // tiles_per_seq, 0, i % tiles_per_seq, 0, 0))
    vt_shape = jax.ShapeDtypeStruct(
        (t // seq, DIFF_HEADS, seq // ATTN_BLOCK, VT_ROWS, ATTN_BLOCK), bf16)
    return pl.pallas_call(
        functools.partial(_inproj_kernel, layer=layer),
        grid=(t // tm,),
        in_specs=[row_spec, _layer_spec(layer, (1, D_MODEL)), _HBM_SPEC,
                  _const_spec((256, 256)), _layer_spec(layer, (1, SGU_WIDTH)),
                  _layer_spec(layer, (SGU_GROUPS, CHUNK, CHUNK)),
                  _layer_spec(layer, (SGU_GROUPS, CHUNK, 1))],
        out_specs=[half_spec,
                   pl.BlockSpec((1, DIFF_WIDTH, tm),
                                lambda i: (i // tiles_per_seq, 0, i % tiles_per_seq)),
                   half_spec, vt_spec],
        out_shape=[half_shape, jax.ShapeDtypeStruct((t // seq, DIFF_WIDTH, seq), bf16),
                   half_shape, vt_shape],
        scratch_shapes=[pltpu.VMEM((D_MODEL, IN_COLS), bf16),
                        pltpu.VMEM((2, D_MODEL, INPROJ_COL_WIDTH), f32),
                        pltpu.SemaphoreType.DMA((2,))],
        compiler_params=pltpu.CompilerParams(
            dimension_semantics=("arbitrary",), vmem_limit_bytes=VMEM_LIMIT_BYTES),
        name="inproj_sgu",
    )(x2d, gain, win, ones, sgain, ws, bs)


def _attn_kernel(lq1_ref, lk1_ref, lq2_ref, lk2_ref, gain_ref, kbias_ref, qt_ref, k_ref, vt_ref,
                 o_ref, qop_ref, m_ref, acc_ref, *, lam_init):
    blk = ATTN_BLOCK
    pair = pl.program_id(1)
    heads = range(DIFF_HEADS)
    slopes = [2.0 ** (-8.0 * (hd + 1) / DIFF_HEADS) * LOG2_E for hd in heads]
    krow = lax.broadcasted_iota(jnp.int32, (blk, 2 * blk), 0)
    qcol = lax.broadcasted_iota(jnp.int32, (blk, 2 * blk), 1) & (blk - 1)
    causal = krow <= qcol

    row = lax.broadcasted_iota(jnp.int32, (DIFF_V_DIM, 2 * blk), 0)
    c1 = jnp.asarray(LOG2_E, f32).astype(bf16).astype(f32)
    q_tail = jnp.where(row == 0, c1, jnp.where(row == 1, LOG2_E - c1, 0.0)).astype(bf16)
    q_zero = jnp.zeros((DIFF_QK_DIM, blk), bf16)
    for qs in range(2):
        for hd in heads:
            qt = qt_ref[0, hd * DIFF_V_DIM:(hd + 1) * DIFF_V_DIM, qs * blk:(qs + 1) * blk]
            qop_ref[qs, hd, :DIFF_QK_DIM, :blk] = qt[:DIFF_QK_DIM]
            qop_ref[qs, hd, :DIFF_QK_DIM, blk:] = q_zero
            qop_ref[qs, hd, DIFF_QK_DIM:DIFF_V_DIM, :blk] = q_zero
            qop_ref[qs, hd, DIFF_QK_DIM:DIFF_V_DIM, blk:] = qt[DIFF_QK_DIM:]
            qop_ref[qs, hd, DIFF_V_DIM:, :] = q_tail

    def scores(item):
        qs, kb, hd, _ = item
        k_rows = k_ref[0, pl.ds(pl.multiple_of(kb * blk, blk), blk),
                       hd * DIFF_V_DIM:(hd + 1) * DIFF_V_DIM]
        return _dot(jnp.concatenate([k_rows, kbias_ref[hd]], axis=1), qop_ref[qs, hd])

    def softmax(item, s):
        qs, kb, hd, diagonal = item
        if diagonal:
            s = jnp.where(causal, s, NEG_INF)
            m_new = jnp.max(s, axis=0, keepdims=True)
            p, alpha = jnp.exp2(s - m_new), None
        else:
            beta = slopes[hd] * ((kb - (2 * pair + qs)) * blk).astype(f32)
            m_old = m_ref[qs, hd]
            m_new = jnp.maximum(m_old, jnp.max(s, axis=0, keepdims=True) + beta)
            p, alpha = jnp.exp2(s - (m_new - beta)), jnp.exp2(m_old - m_new)
        m_ref[qs, hd] = m_new
        return item, p.astype(bf16), alpha

    def accumulate(item, p, alpha):
        qs, kb, hd, _ = item
        pv = _dot(vt_ref[0, hd, kb], p)
        acc_ref[qs, hd] = pv if alpha is None else alpha * acc_ref[qs, hd] + pv

    def run(items):
        scored = [scores(item) for item in items[:ATTN_LOOKAHEAD]]
        weighted = []
        for n, item in enumerate(items):
            if n + ATTN_LOOKAHEAD < len(items):
                scored.append(scores(items[n + ATTN_LOOKAHEAD]))
            weighted.append(softmax(item, scored.pop(0)))
            if len(weighted) > ATTN_PV_DELAY:
                accumulate(*weighted.pop(0))
        for args in weighted:
            accumulate(*args)

    first, second = 2 * pair, 2 * pair + 1
    run([(0, first, hd, True) for hd in heads] + [(1, second, hd, True) for hd in heads]
        + [(1, first, hd, False) for hd in heads])

    def earlier_keys(j, carry):
        run([(qs, 2 * j + kbo, hd, False) for kbo in range(2) for qs in range(2) for hd in heads])
        return carry

    lax.fori_loop(0, pair, earlier_keys, 0)

    lam = (jnp.exp(jnp.sum(lq1_ref[...] * lk1_ref[...], axis=-1, keepdims=True))
           - jnp.exp(jnp.sum(lq2_ref[...] * lk2_ref[...], axis=-1, keepdims=True))
           + lam_init)
    gain = gain_ref[...] * (1.0 - lam_init)
    for qs in range(2):
        for hd in heads:
            acc1, acc2 = acc_ref[qs, hd, :, :blk], acc_ref[qs, hd, :, blk:]
            o = (acc1[:DIFF_V_DIM] / acc1[DIFF_V_DIM:DIFF_V_DIM + 1]
                 - lam * (acc2[:DIFF_V_DIM] / acc2[DIFF_V_DIM:DIFF_V_DIM + 1]))
            ms = jnp.mean(o * o, axis=0, keepdims=True)
            o_ref[0, hd * DIFF_V_DIM:(hd + 1) * DIFF_V_DIM, qs * blk:(qs + 1) * blk] = (
                o * lax.rsqrt(ms + EPS) * gain).astype(bf16)


def _attn_call(layer, lq1, lk1, lq2, lk2, gain, kbias, qt, k, vt, lam_init):
    b, s, _ = k.shape
    blk = ATTN_BLOCK
    lam_spec = _layer_spec(layer, (1, DIFF_QK_DIM))
    q_spec = pl.BlockSpec((1, DIFF_WIDTH, 2 * blk), lambda bi, pi: (bi, 0, pi))
    k_spec = pl.BlockSpec((1, s, DIFF_WIDTH), lambda bi, pi: (bi, 0, 0))
    vt_spec = pl.BlockSpec((1, DIFF_HEADS, s // blk, VT_ROWS, blk),
                           lambda bi, pi: (bi, 0, 0, 0, 0))
    return pl.pallas_call(
        functools.partial(_attn_kernel, lam_init=lam_init),
        grid=(b, s // (2 * blk)),
        in_specs=[lam_spec, lam_spec, lam_spec, lam_spec, _layer_spec(layer, (DIFF_V_DIM, 1)),
                  _const_spec((DIFF_HEADS, blk, DIFF_V_DIM)), q_spec, k_spec, vt_spec],
        out_specs=pl.BlockSpec((1, DIFF_WIDTH, 2 * blk), lambda bi, pi: (bi, 0, pi)),
        out_shape=jax.ShapeDtypeStruct((b, DIFF_WIDTH, s), bf16),
        scratch_shapes=[pltpu.VMEM((2, DIFF_HEADS, 2 * DIFF_V_DIM, 2 * blk), bf16),
                        pltpu.VMEM((2, DIFF_HEADS, 1, 2 * blk), f32),
                        pltpu.VMEM((2, DIFF_HEADS, VT_ROWS, 2 * blk), f32)],
        compiler_params=pltpu.CompilerParams(
            dimension_semantics=("parallel", "arbitrary"),
            vmem_limit_bytes=VMEM_LIMIT_BYTES),
        name="diff_attn",
    )(lq1, lk1, lq2, lk2, gain, kbias, qt, k, vt)


def _alibi_key_columns():
    r = jnp.arange(ATTN_BLOCK, dtype=f32)
    slopes = jnp.exp2(-8.0 * jnp.arange(1, DIFF_HEADS + 1, dtype=f32) / DIFF_HEADS)
    cols = jnp.zeros((DIFF_HEADS, ATTN_BLOCK, DIFF_V_DIM), f32)
    cols = cols.at[:, :, 0:2].set((slopes[:, None] * r[None, :])[:, :, None])
    return cols.astype(bf16)


def kernel(x, ffn1_norm, ffn1_w_gate, ffn1_w_up, ffn1_w_down, mix_norm, w_in, sgu_norm, sgu_w, sgu_b, lambda_q1, lambda_k1, lambda_q2, lambda_k2, diff_subln, w_out, ffn2_norm, ffn2_w_gate, ffn2_w_up, ffn2_w_down, final_norm):
    b, s, d = x.shape
    depth = w_in.shape[0]
    t = b * s
    x2d = x.reshape(t, d)
    group_id = jnp.arange(256) // SGU_GROUP_DIM
    ones = (group_id[:, None] == group_id[None, :]).astype(bf16)
    kbias = _alibi_key_columns()

    as_rows = lambda p: p.reshape(depth, 1, -1)
    wg1, wu1, wd1 = ffn1_w_gate, ffn1_w_up, ffn1_w_down
    wg2, wu2, wd2 = ffn2_w_gate, ffn2_w_up, ffn2_w_down
    win = w_in
    wo = w_out
    sgu_b_cols = sgu_b.reshape(depth, SGU_GROUPS, CHUNK, 1)
    subln_cols = diff_subln.reshape(depth, DIFF_V_DIM, 1)

    for l in range(depth):
        x2d = _ffn_call(l, x2d, as_rows(ffn1_norm), wg1, wu1, wd1)
        ya, qt, k, vt = _inproj_call(l, x2d, s, as_rows(mix_norm), win, ones,
                                    as_rows(sgu_norm), sgu_w, sgu_b_cols)
        lam_init = 0.8 - 0.6 * math.exp(-0.3 * l)
        ybt = _attn_call(l, as_rows(lambda_q1), as_rows(lambda_k1), as_rows(lambda_q2),
                         as_rows(lambda_k2), subln_cols, kbias,
                         qt, k.reshape(b, s, DIFF_WIDTH), vt, lam_init)
        x2d = _mix_ffn_call(l, x2d, ya, ybt, wo, as_rows(ffn2_norm),
                            wg2, wu2, wd2, final_norm.reshape(1, d),
                            final_norm=(l == depth - 1))
    return x2d.reshape(b, s, d)
```

```python
import functools
import math

import jax
import jax.numpy as jnp
from jax import lax
from jax.experimental import pallas as pl
from jax.experimental.pallas import tpu as pltpu

D_MODEL = 1024
D_FF = 2816
SGU_WIDTH = 512
SGU_GROUP_DIM = 64
SGU_GROUPS = SGU_WIDTH // SGU_GROUP_DIM
CHUNK = 128
DIFF_WIDTH = 512
DIFF_V_DIM = 128
DIFF_QK_DIM = 64
DIFF_HEADS = DIFF_WIDTH // DIFF_V_DIM
IN_COLS = 2 * SGU_WIDTH + 3 * DIFF_WIDTH
EPS = 1e-6
NEG_INF = -1e30
LOG2_E = math.log2(math.e)

VMEM_LIMIT_BYTES = 56 * 1024 * 1024

FFN_TOKENS = 512
FFN_CHUNKS = ((0, 512), (512, 1024), (1024, 1536), (1536, 2048), (2048, 2560), (2560, 2816))
FFN_CHUNK_MAX = max(c1 - c0 for c0, c1 in FFN_CHUNKS)
assert FFN_CHUNK_MAX == SGU_WIDTH == DIFF_WIDTH
INPROJ_TOKENS = 1024
ATTN_BLOCK = 256
VT_ROWS = DIFF_V_DIM + 16
ATTN_LOOKAHEAD = 3
ATTN_PV_DELAY = 2

bf16 = jnp.bfloat16
f32 = jnp.float32


def _rmsnorm_rows(x, gain):
    ms = jnp.mean(x * x, axis=-1, keepdims=True)
    return x * lax.rsqrt(ms + EPS) * gain


def _dot(a, b):
    return jnp.dot(a, b, preferred_element_type=f32)


def _dot_nt(a, b):
    return lax.dot_general(a, b, (((1,), (1,)), ((), ())), preferred_element_type=f32)


def _dot_tn(a, b):
    return lax.dot_general(a, b, (((0,), (0,)), ((), ())), preferred_element_type=f32)


class _FfnWeights:
    def __init__(self, layer, hbm, resident, stage_cols, stage_rows, sems):
        self.layer = layer
        self.wg_hbm, self.wu_hbm, self.wd_hbm = hbm
        self.wg, self.wu, self.wd = resident
        self.stage_cols, self.stage_rows, self.sems = stage_cols, stage_rows, sems

    def copies(self, c, kinds=(0, 1, 2)):
        c0, c1 = FFN_CHUNKS[c]
        w, slot = c1 - c0, c % 2
        endpoints = (
            (self.wg_hbm.at[self.layer, :, pl.ds(c0, w)], self.stage_cols.at[slot, 0, :, pl.ds(0, w)]),
            (self.wu_hbm.at[self.layer, :, pl.ds(c0, w)], self.stage_cols.at[slot, 1, :, pl.ds(0, w)]),
            (self.wd_hbm.at[self.layer, pl.ds(c0, w), :], self.stage_rows.at[slot, pl.ds(0, w), :]))
        return [pltpu.make_async_copy(*endpoints[k], self.sems.at[slot, k]) for k in kinds]

    def land(self, c):
        c0, c1 = FFN_CHUNKS[c]
        w, slot = c1 - c0, c % 2
        for copy in self.copies(c):
            copy.wait()
        self.wg[:, c0:c1] = self.stage_cols[slot, 0, :, :w].astype(bf16)
        self.wu[:, c0:c1] = self.stage_cols[slot, 1, :, :w].astype(bf16)
        self.wd[c0:c1, :] = self.stage_rows[slot, :w, :].astype(bf16)
        if c + 2 < len(FFN_CHUNKS):
            for copy in self.copies(c + 2):
                copy.start()


def _swiglu_half_step(x, gain, weights, landing):
    h = _rmsnorm_rows(x, gain).astype(bf16)
    acc = None
    for c, (c0, c1) in enumerate(FFN_CHUNKS):
        if landing:
            weights.land(c)
        g = _dot(h, weights.wg[:, c0:c1])
        u = _dot(h, weights.wu[:, c0:c1])
        a = (g / (1.0 + jnp.exp(-g)) * u).astype(bf16)
        d = _dot(a, weights.wd[c0:c1, :])
        acc = d if acc is None else acc + d
    return x + 0.5 * acc


def _ffn_kernel(x_ref, gain_ref, wg_hbm, wu_hbm, wd_hbm, o_ref,
                wg_ref, wu_ref, wd_ref, stage_cols, stage_rows, sems, *, layer):
    first_step = pl.program_id(0) == 0
    weights = _FfnWeights(layer, (wg_hbm, wu_hbm, wd_hbm), (wg_ref, wu_ref, wd_ref),
                          stage_cols, stage_rows, sems)

    @pl.when(first_step)
    def _():
        for copy in weights.copies(0) + weights.copies(1):
            copy.start()
        o_ref[...] = _swiglu_half_step(x_ref[...], gain_ref[...], weights, landing=True)

    @pl.when(jnp.logical_not(first_step))
    def _():
        o_ref[...] = _swiglu_half_step(x_ref[...], gain_ref[...], weights, landing=False)


def _mix_ffn_kernel(x_ref, ya_ref, ybt_ref, wo_hbm, gain_ref, wg_hbm, wu_hbm, wd_hbm, fgain_ref,
                    o_ref, wo_ref, wg_ref, wu_ref, wd_ref, stage_cols, stage_rows, sems,
                    *, layer, final_norm):
    first_step = pl.program_id(0) == 0
    weights = _FfnWeights(layer, (wg_hbm, wu_hbm, wd_hbm), (wg_ref, wu_ref, wd_ref),
                          stage_cols, stage_rows, sems)

    def stage_out_projection():
        halves = [pltpu.make_async_copy(wo_hbm.at[layer, pl.ds(s * SGU_WIDTH, SGU_WIDTH), :],
                                        stage_rows.at[s], sems.at[s, 2]) for s in range(2)]
        for copy in halves + weights.copies(0, (0, 1)) + weights.copies(1, (0, 1)):
            copy.start()
        for s, copy in enumerate(halves):
            copy.wait()
            wo_ref[s * SGU_WIDTH:(s + 1) * SGU_WIDTH, :] = stage_rows[s].astype(bf16)
        for copy in weights.copies(0, (2,)) + weights.copies(1, (2,)):
            copy.start()

    def step(landing):
        if landing:
            stage_out_projection()
        x = x_ref[...]
        x = (x + _dot(ya_ref[...], wo_ref[:SGU_WIDTH, :])
             + _dot_tn(ybt_ref[0], wo_ref[SGU_WIDTH:, :]))
        x = _swiglu_half_step(x, gain_ref[...], weights, landing)
        if final_norm:
            x = _rmsnorm_rows(x, fgain_ref[...])
        o_ref[...] = x

    pl.when(first_step)(functools.partial(step, True))
    pl.when(jnp.logical_not(first_step))(functools.partial(step, False))


def _const_spec(shape):
    nd = len(shape)
    return pl.BlockSpec(shape, lambda *_: (0,) * nd, pipeline_mode=pl.Buffered(1))


def _layer_spec(layer, shape):
    nd = len(shape)
    return pl.BlockSpec((None,) + tuple(shape), lambda *_: (layer,) + (0,) * nd,
                        pipeline_mode=pl.Buffered(1))


_HBM_SPEC = pl.BlockSpec(memory_space=pl.ANY)


def _ffn_weight_scratch():
    return [pltpu.VMEM((D_MODEL, D_FF), bf16), pltpu.VMEM((D_MODEL, D_FF), bf16),
            pltpu.VMEM((D_FF, D_MODEL), bf16),
            pltpu.VMEM((2, 2, D_MODEL, FFN_CHUNK_MAX), f32),
            pltpu.VMEM((2, FFN_CHUNK_MAX, D_MODEL), f32),
            pltpu.SemaphoreType.DMA((2, 3))]


def _ffn_call(layer, x2d, gain, wg, wu, wd):
    t = x2d.shape[0]
    tm = FFN_TOKENS
    row_spec = pl.BlockSpec((tm, D_MODEL), lambda i: (i, 0))
    return pl.pallas_call(
        functools.partial(_ffn_kernel, layer=layer),
        grid=(t // tm,),
        in_specs=[row_spec, _layer_spec(layer, (1, D_MODEL)), _HBM_SPEC, _HBM_SPEC, _HBM_SPEC],
        out_specs=row_spec,
        out_shape=jax.ShapeDtypeStruct((t, D_MODEL), f32),
        scratch_shapes=_ffn_weight_scratch(),
        compiler_params=pltpu.CompilerParams(
            dimension_semantics=("arbitrary",), vmem_limit_bytes=VMEM_LIMIT_BYTES),
        name="ffn",
    )(x2d, gain, wg, wu, wd)


def _mix_ffn_call(layer, x2d, ya, ybt, wo, gain, wg, wu, wd, fgain, final_norm):
    t = x2d.shape[0]
    tm = FFN_TOKENS
    tiles_per_seq = ybt.shape[2] // tm
    row_spec = pl.BlockSpec((tm, D_MODEL), lambda i: (i, 0))
    half_spec = pl.BlockSpec((tm, SGU_WIDTH), lambda i: (i, 0))
    ybt_spec = pl.BlockSpec((1, DIFF_WIDTH, tm),
                            lambda i: (i // tiles_per_seq, 0, i % tiles_per_seq))
    return pl.pallas_call(
        functools.partial(_mix_ffn_kernel, layer=layer, final_norm=final_norm),
        grid=(t // tm,),
        in_specs=[row_spec, half_spec, ybt_spec, _HBM_SPEC, _layer_spec(layer, (1, D_MODEL)),
                  _HBM_SPEC, _HBM_SPEC, _HBM_SPEC, _const_spec((1, D_MODEL))],
        out_specs=row_spec,
        out_shape=jax.ShapeDtypeStruct((t, D_MODEL), f32),
        scratch_shapes=[pltpu.VMEM((2 * SGU_WIDTH, D_MODEL), bf16)] + _ffn_weight_scratch(),
        compiler_params=pltpu.CompilerParams(
            dimension_semantics=("arbitrary",), vmem_limit_bytes=VMEM_LIMIT_BYTES),
        name="mix_ffn",
    )(x2d, ya, ybt, wo, gain, wg, wu, wd, fgain)


def _gelu_exact(x):
    return 0.5 * x * (1.0 + lax.erf(x * (1.0 / math.sqrt(2.0))))


_COL_ZV, _COL_ZU, _COL_Q, _COL_K, _COL_V = (
    SGU_WIDTH, 0, 2 * SGU_WIDTH, 2 * SGU_WIDTH + DIFF_WIDTH, 2 * SGU_WIDTH + 2 * DIFF_WIDTH)
INPROJ_COL_ORDER = (_COL_ZV, _COL_ZU, _COL_Q, _COL_K, _COL_V)
INPROJ_COL_WIDTH = 512
assert SGU_WIDTH == DIFF_WIDTH == INPROJ_COL_WIDTH


def _inproj_kernel(x_ref, gain_ref, win_hbm, ones_ref, sgain_ref, ws_ref, bs_ref,
                   ya_ref, qt_ref, k_ref, vt_ref, win_ref, stage_ref, sems, *, layer):
    tm = x_ref.shape[0]
    width = INPROJ_COL_WIDTH
    first_step = pl.program_id(0) == 0

    def copy(n):
        return pltpu.make_async_copy(
            win_hbm.at[layer, :, pl.ds(INPROJ_COL_ORDER[n], width)],
            stage_ref.at[n % 2], sems.at[n % 2])

    def step(landing):
        def columns(n):
            c0 = INPROJ_COL_ORDER[n]
            if landing:
                copy(n).wait()
                win_ref[:, c0:c0 + width] = stage_ref[n % 2].astype(bf16)
                if n + 2 < len(INPROJ_COL_ORDER):
                    copy(n + 2).start()
            return win_ref[:, c0:c0 + width]

        if landing:
            copy(0).start()
            copy(1).start()
        h = _rmsnorm_rows(x_ref[...], gain_ref[...]).astype(bf16)

        v = _gelu_exact(_dot(h, columns(0)))
        zu = _dot(h, columns(1))
        v2 = v * v
        v2_hi = v2.astype(bf16)
        v2_lo = (v2 - v2_hi.astype(f32)).astype(bf16)
        ones = ones_ref[...]
        ms = jnp.concatenate(
            [_dot(v2_hi[:, c:c + 256], ones) + _dot(v2_lo[:, c:c + 256], ones)
             for c in range(0, SGU_WIDTH, 256)], axis=-1) * (1.0 / SGU_GROUP_DIM)
        qt = _dot_nt_t(columns(2), h)
        u = _gelu_exact(zu)
        vn = (v * lax.rsqrt(ms + EPS) * sgain_ref[...]).astype(bf16)
        k = _dot(h, columns(3))
        qt_ref[0] = (qt * (DIFF_QK_DIM ** -0.5 * LOG2_E)).astype(bf16)

        row = lax.broadcasted_iota(jnp.int32, (CHUNK, CHUNK), 0)
        col = lax.broadcasted_iota(jnp.int32, (CHUNK, CHUNK), 1)
        causal = col <= row
        gates = []
        for g in range(SGU_GROUPS):
            w = jnp.where(causal, ws_ref[g], 0.0).astype(bf16)
            lo, hi = g * SGU_GROUP_DIM, (g + 1) * SGU_GROUP_DIM
            gates.append([_dot(w, vn[c * CHUNK:(c + 1) * CHUNK, lo:hi]) + bs_ref[g]
                          for c in range(tm // CHUNK)])
        k_ref[...] = k.astype(bf16)

        vt = _dot_nt_t(columns(4), h)
        for g in range(SGU_GROUPS):
            lo, hi = g * SGU_GROUP_DIM, (g + 1) * SGU_GROUP_DIM
            for c in range(tm // CHUNK):
                r0, r1 = c * CHUNK, (c + 1) * CHUNK
                ya_ref[r0:r1, lo:hi] = (u[r0:r1, lo:hi] * gates[g][c]).astype(bf16)

        vt = vt.astype(bf16)
        ones_rows = jnp.ones((VT_ROWS - DIFF_V_DIM, ATTN_BLOCK), bf16)
        for hd in range(DIFF_HEADS):
            for c in range(tm // ATTN_BLOCK):
                vt_ref[0, hd, c, :DIFF_V_DIM, :] = vt[hd * DIFF_V_DIM:(hd + 1) * DIFF_V_DIM,
                                                      c * ATTN_BLOCK:(c + 1) * ATTN_BLOCK]
                vt_ref[0, hd, c, DIFF_V_DIM:, :] = ones_rows

    pl.when(first_step)(functools.partial(step, True))
    pl.when(jnp.logical_not(first_step))(functools.partial(step, False))


def _dot_nt_t(w, h):
    return lax.dot_general(w, h, (((0,), (1,)), ((), ())), preferred_element_type=f32)


def _inproj_call(layer, x2d, seq, gain, win, ones, sgain, ws, bs):
    t = x2d.shape[0]
    tm = INPROJ_TOKENS
    tiles_per_seq = seq // tm
    row_spec = pl.BlockSpec((tm, D_MODEL), lambda i: (i, 0))
    half_spec = pl.BlockSpec((tm, SGU_WIDTH), lambda i: (i, 0))
    half_shape = jax.ShapeDtypeStruct((t, SGU_WIDTH), bf16)
    vt_spec = pl.BlockSpec((1, DIFF_HEADS, tm // ATTN_BLOCK, VT_ROWS, ATTN_BLOCK),
                           lambda i: (i // tiles_per_seq, 0, i % tiles_per_seq, 0, 0))
    vt_shape = jax.ShapeDtypeStruct(
        (t // seq, DIFF_HEADS, seq // ATTN_BLOCK, VT_ROWS, ATTN_BLOCK), bf16)
    return pl.pallas_call(
        functools.partial(_inproj_kernel, layer=layer),
        grid=(t // tm,),
        in_specs=[row_spec, _layer_spec(layer, (1, D_MODEL)), _HBM_SPEC,
                  _const_spec((256, 256)), _layer_spec(layer, (1, SGU_WIDTH)),
                  _layer_spec(layer, (SGU_GROUPS, CHUNK, CHUNK)),
                  _layer_spec(layer, (SGU_GROUPS, CHUNK, 1))],
        out_specs=[half_spec,
                   pl.BlockSpec((1, DIFF_WIDTH, tm),
                                lambda i: (i // tiles_per_seq, 0, i % tiles_per_seq)),
                   half_spec, vt_spec],
        out_shape=[half_shape, jax.ShapeDtypeStruct((t // seq, DIFF_WIDTH, seq), bf16),
                   half_shape, vt_shape],
        scratch_shapes=[pltpu.VMEM((D_MODEL, IN_COLS), bf16),
                        pltpu.VMEM((2, D_MODEL, INPROJ_COL_WIDTH), f32),
                        pltpu.SemaphoreType.DMA((2,))],
        compiler_params=pltpu.CompilerParams(
            dimension_semantics=("arbitrary",), vmem_limit_bytes=VMEM_LIMIT_BYTES),
        name="inproj_sgu",
    )(x2d, gain, win, ones, sgain, ws, bs)


def _attn_kernel(lq1_ref, lk1_ref, lq2_ref, lk2_ref, gain_ref, kbias_ref, qt_ref, k_ref, vt_ref,
                 o_ref, qop_ref, m_ref, acc_ref, *, lam_init):
    blk = ATTN_BLOCK
    pair = pl.program_id(1)
    heads = range(DIFF_HEADS)
    slopes = [2.0 ** (-8.0 * (hd + 1) / DIFF_HEADS) * LOG2_E for hd in heads]

    row = lax.broadcasted_iota(jnp.int32, (DIFF_V_DIM, 2 * blk), 0)
    c1 = jnp.asarray(LOG2_E, f32).astype(bf16).astype(f32)
    q_tail = jnp.where(row >= 4, 0.0, jnp.where(row % 2 == 0, c1, LOG2_E - c1)).astype(bf16)
    q_zero = jnp.zeros((DIFF_QK_DIM, blk), bf16)
    for qs in range(2):
        for hd in heads:
            qt = qt_ref[0, hd * DIFF_V_DIM:(hd + 1) * DIFF_V_DIM, qs * blk:(qs + 1) * blk]
            qop_ref[qs, hd, :DIFF_QK_DIM, :blk] = qt[:DIFF_QK_DIM]
            qop_ref[qs, hd, :DIFF_QK_DIM, blk:] = q_zero
            qop_ref[qs, hd, DIFF_QK_DIM:DIFF_V_DIM, :blk] = q_zero
            qop_ref[qs, hd, DIFF_QK_DIM:DIFF_V_DIM, blk:] = qt[DIFF_QK_DIM:]
            qop_ref[qs, hd, DIFF_V_DIM:, :] = q_tail

    def scores(item):
        qs, kb, hd, n_blocks, diagonal = item
        n = n_blocks * blk
        k_rows = k_ref[0, pl.ds(pl.multiple_of(kb * blk, blk), n),
                       hd * DIFF_V_DIM:(hd + 1) * DIFF_V_DIM]
        s = _dot(jnp.concatenate([k_rows, kbias_ref[hd, :n]], axis=1), qop_ref[qs, hd])
        if diagonal:
            krow = lax.broadcasted_iota(jnp.int32, s.shape, 0)
            qcol = lax.broadcasted_iota(jnp.int32, s.shape, 1) & (blk - 1)
            s = jnp.where(krow <= qcol + (n - blk), s, NEG_INF)
        return s, jnp.max(s, axis=0, keepdims=True)

    def softmax(item, scored):
        qs, kb, hd, n_blocks, diagonal = item
        s, s_max = scored
        if diagonal:
            beta = -slopes[hd] * blk * (n_blocks - 1)
            m_new = s_max + beta
            p, alpha = jnp.exp2(s - s_max), None
        else:
            beta = slopes[hd] * ((kb - (2 * pair + qs)) * blk).astype(f32)
            m_old = m_ref[qs, hd]
            m_new = jnp.maximum(m_old, s_max + beta)
            p, alpha = jnp.exp2(s - (m_new - beta)), jnp.exp2(m_old - m_new)
        m_ref[qs, hd] = m_new
        return item, p.astype(bf16), alpha

    def accumulate(item, p, alpha):
        qs, kb, hd, n_blocks, _ = item
        vt = jnp.concatenate([vt_ref[0, hd, kb + i] for i in range(n_blocks)], axis=1)
        pv = _dot(vt, p)
        acc_ref[qs, hd] = pv if alpha is None else alpha * acc_ref[qs, hd] + pv

    def run(items):
        scored = [scores(item) for item in items[:ATTN_LOOKAHEAD]]
        weighted = []
        for n, item in enumerate(items):
            if n + ATTN_LOOKAHEAD < len(items):
                scored.append(scores(items[n + ATTN_LOOKAHEAD]))
            weighted.append(softmax(item, scored.pop(0)))
            if len(weighted) > ATTN_PV_DELAY:
                accumulate(*weighted.pop(0))
        for args in weighted:
            accumulate(*args)

    first = 2 * pair
    run([(0, first, hd, 1, True) for hd in heads] + [(1, first, hd, 2, True) for hd in heads])

    def earlier_keys(j, carry):
        run([(qs, 2 * j, hd, 2, False) for qs in range(2) for hd in heads])
        return carry

    lax.fori_loop(0, pair, earlier_keys, 0)

    lam = (jnp.exp(jnp.sum(lq1_ref[...] * lk1_ref[...], axis=-1, keepdims=True))
           - jnp.exp(jnp.sum(lq2_ref[...] * lk2_ref[...], axis=-1, keepdims=True))
           + lam_init)
    gain = gain_ref[...] * (1.0 - lam_init)
    for qs in range(2):
        for hd in heads:
            acc1, acc2 = acc_ref[qs, hd, :, :blk], acc_ref[qs, hd, :, blk:]
            o = (acc1[:DIFF_V_DIM] / acc1[DIFF_V_DIM:DIFF_V_DIM + 1]
                 - lam * (acc2[:DIFF_V_DIM] / acc2[DIFF_V_DIM:DIFF_V_DIM + 1]))
            ms = jnp.mean(o * o, axis=0, keepdims=True)
            o_ref[0, hd * DIFF_V_DIM:(hd + 1) * DIFF_V_DIM, qs * blk:(qs + 1) * blk] = (
                o * lax.rsqrt(ms + EPS) * gain).astype(bf16)


def _attn_call(layer, lq1, lk1, lq2, lk2, gain, kbias, qt, k, vt, lam_init):
    b, s, _ = k.shape
    blk = ATTN_BLOCK
    lam_spec = _layer_spec(layer, (1, DIFF_QK_DIM))
    q_spec = pl.BlockSpec((1, DIFF_WIDTH, 2 * blk), lambda bi, pi: (bi, 0, pi))
    k_spec = pl.BlockSpec((1, s, DIFF_WIDTH), lambda bi, pi: (bi, 0, 0))
    vt_spec = pl.BlockSpec((1, DIFF_HEADS, s // blk, VT_ROWS, blk),
                           lambda bi, pi: (bi, 0, 0, 0, 0))
    return pl.pallas_call(
        functools.partial(_attn_kernel, lam_init=lam_init),
        grid=(b, s // (2 * blk)),
        in_specs=[lam_spec, lam_spec, lam_spec, lam_spec, _layer_spec(layer, (DIFF_V_DIM, 1)),
                  _const_spec((DIFF_HEADS, 2 * blk, DIFF_V_DIM)), q_spec, k_spec, vt_spec],
        out_specs=pl.BlockSpec((1, DIFF_WIDTH, 2 * blk), lambda bi, pi: (bi, 0, pi)),
        out_shape=jax.ShapeDtypeStruct((b, DIFF_WIDTH, s), bf16),
        scratch_shapes=[pltpu.VMEM((2, DIFF_HEADS, 2 * DIFF_V_DIM, 2 * blk), bf16),
                        pltpu.VMEM((2, DIFF_HEADS, 1, 2 * blk), f32),
                        pltpu.VMEM((2, DIFF_HEADS, VT_ROWS, 2 * blk), f32)],
        compiler_params=pltpu.CompilerParams(
            dimension_semantics=("parallel", "arbitrary"),
            vmem_limit_bytes=VMEM_LIMIT_BYTES),
        name="diff_attn",
    )(lq1, lk1, lq2, lk2, gain, kbias, qt, k, vt)


def _alibi_key_columns():
    r = jnp.arange(2 * ATTN_BLOCK)
    slopes = jnp.exp2(-8.0 * jnp.arange(1, DIFF_HEADS + 1, dtype=f32) / DIFF_HEADS)
    low = (r % ATTN_BLOCK).astype(f32)
    high = (r - r % ATTN_BLOCK).astype(f32)
    cols = jnp.zeros((DIFF_HEADS, 2 * ATTN_BLOCK, DIFF_V_DIM), f32)
    cols = cols.at[:, :, 0:2].set((slopes[:, None] * low[None, :])[:, :, None])
    cols = cols.at[:, :, 2:4].set((slopes[:, None] * high[None, :])[:, :, None])
    return cols.astype(bf16)


def kernel(x, ffn1_norm, ffn1_w_gate, ffn1_w_up, ffn1_w_down, mix_norm, w_in, sgu_norm, sgu_w, sgu_b, lambda_q1, lambda_k1, lambda_q2, lambda_k2, diff_subln, w_out, ffn2_norm, ffn2_w_gate, ffn2_w_up, ffn2_w_down, final_norm):
    b, s, d = x.shape
    depth = w_in.shape[0]
    t = b * s
    x2d = x.reshape(t, d)
    group_id = jnp.arange(256) // SGU_GROUP_DIM
    ones = (group_id[:, None] == group_id[None, :]).astype(bf16)
    kbias = _alibi_key_columns()

    as_rows = lambda p: p.reshape(depth, 1, -1)
    wg1, wu1, wd1 = ffn1_w_gate, ffn1_w_up, ffn1_w_down
    wg2, wu2, wd2 = ffn2_w_gate, ffn2_w_up, ffn2_w_down
    win = w_in
    wo = w_out
    sgu_b_cols = sgu_b.reshape(depth, SGU_GROUPS, CHUNK, 1)
    subln_cols = diff_subln.reshape(depth, DIFF_V_DIM, 1)

    for l in range(depth):
        x2d = _ffn_call(l, x2d, as_rows(ffn1_norm), wg1, wu1, wd1)
        ya, qt, k, vt = _inproj_call(l, x2d, s, as_rows(mix_norm), win, ones,
                                    as_rows(sgu_norm), sgu_w, sgu_b_cols)
        lam_init = 0.8 - 0.6 * math.exp(-0.3 * l)
        ybt = _attn_call(l, as_rows(lambda_q1), as_rows(lambda_k1), as_rows(lambda_q2),
                         as_rows(lambda_k2), subln_cols, kbias,
                         qt, k.reshape(b, s, DIFF_WIDTH), vt, lam_init)
        x2d = _mix_ffn_call(l, x2d, ya, ybt, wo, as_rows(ffn2_norm),
                            wg2, wu2, wd2, final_norm.reshape(1, d),
                            final_norm=(l == depth - 1))
    return x2d.reshape(b, s, d)
```

```python
import functools
import math

import jax
import jax.numpy as jnp
from jax import lax
from jax.experimental import pallas as pl
from jax.experimental.pallas import tpu as pltpu

D_MODEL = 1024
D_FF = 2816
SGU_WIDTH = 512
SGU_GROUP_DIM = 64
SGU_GROUPS = SGU_WIDTH // SGU_GROUP_DIM
CHUNK = 128
DIFF_WIDTH = 512
DIFF_V_DIM = 128
DIFF_QK_DIM = 64
DIFF_HEADS = DIFF_WIDTH // DIFF_V_DIM
IN_COLS = 2 * SGU_WIDTH + 3 * DIFF_WIDTH
EPS = 1e-6
NEG_INF = -1e30
LOG2_E = math.log2(math.e)

VMEM_LIMIT_BYTES = 56 * 1024 * 1024

FFN_TOKENS = 512
FFN_CHUNKS = ((0, 512), (512, 1024), (1024, 1536), (1536, 2048), (2048, 2560), (2560, 2816))
FFN_CHUNK_MAX = max(c1 - c0 for c0, c1 in FFN_CHUNKS)
assert FFN_CHUNK_MAX == SGU_WIDTH == DIFF_WIDTH
INPROJ_TOKENS = 1024
ATTN_BLOCK = 256
VT_ROWS = DIFF_V_DIM + 16
ATTN_LOOKAHEAD = 3
ATTN_PV_DELAY = 2

bf16 = jnp.bfloat16
f32 = jnp.float32


def _rmsnorm_rows(x, gain):
    ms = jnp.mean(x * x, axis=-1, keepdims=True)
    return x * lax.rsqrt(ms + EPS) * gain


def _dot(a, b):
    return jnp.dot(a, b, preferred_element_type=f32)


def _dot_nt(a, b):
    return lax.dot_general(a, b, (((1,), (1,)), ((), ())), preferred_element_type=f32)


def _dot_tn(a, b):
    return lax.dot_general(a, b, (((0,), (0,)), ((), ())), preferred_element_type=f32)


class _FfnWeights:
    def __init__(self, layer, hbm, resident, stage_cols, stage_rows, sems):
        self.layer = layer
        self.wg_hbm, self.wu_hbm, self.wd_hbm = hbm
        self.wg, self.wu, self.wd = resident
        self.stage_cols, self.stage_rows, self.sems = stage_cols, stage_rows, sems

    def copies(self, c, kinds=(0, 1, 2)):
        c0, c1 = FFN_CHUNKS[c]
        w, slot = c1 - c0, c % 2
        endpoints = (
            (self.wg_hbm.at[self.layer, :, pl.ds(c0, w)], self.stage_cols.at[slot, 0, :, pl.ds(0, w)]),
            (self.wu_hbm.at[self.layer, :, pl.ds(c0, w)], self.stage_cols.at[slot, 1, :, pl.ds(0, w)]),
            (self.wd_hbm.at[self.layer, pl.ds(c0, w), :], self.stage_rows.at[slot, pl.ds(0, w), :]))
        return [pltpu.make_async_copy(*endpoints[k], self.sems.at[slot, k]) for k in kinds]

    def land(self, c):
        c0, c1 = FFN_CHUNKS[c]
        w, slot = c1 - c0, c % 2
        for copy in self.copies(c):
            copy.wait()
        self.wg[:, c0:c1] = self.stage_cols[slot, 0, :, :w].astype(bf16)
        self.wu[:, c0:c1] = self.stage_cols[slot, 1, :, :w].astype(bf16)
        self.wd[c0:c1, :] = self.stage_rows[slot, :w, :].astype(bf16)
        if c + 2 < len(FFN_CHUNKS):
            for copy in self.copies(c + 2):
                copy.start()


def _swiglu_half_step(x, gain, weights, landing):
    h = _rmsnorm_rows(x, gain).astype(bf16)
    acc = None
    for c, (c0, c1) in enumerate(FFN_CHUNKS):
        if landing:
            weights.land(c)
        g = _dot(h, weights.wg[:, c0:c1])
        u = _dot(h, weights.wu[:, c0:c1])
        a = (g / (1.0 + jnp.exp(-g)) * u).astype(bf16)
        d = _dot(a, weights.wd[c0:c1, :])
        acc = d if acc is None else acc + d
    return x + 0.5 * acc


def _ffn_kernel(x_ref, gain_ref, wg_hbm, wu_hbm, wd_hbm, o_ref,
                wg_ref, wu_ref, wd_ref, stage_cols, stage_rows, sems, *, layer):
    first_step = pl.program_id(0) == 0
    weights = _FfnWeights(layer, (wg_hbm, wu_hbm, wd_hbm), (wg_ref, wu_ref, wd_ref),
                          stage_cols, stage_rows, sems)

    @pl.when(first_step)
    def _():
        for copy in weights.copies(0) + weights.copies(1):
            copy.start()
        o_ref[...] = _swiglu_half_step(x_ref[...], gain_ref[...], weights, landing=True)

    @pl.when(jnp.logical_not(first_step))
    def _():
        o_ref[...] = _swiglu_half_step(x_ref[...], gain_ref[...], weights, landing=False)


def _mix_ffn_kernel(x_ref, yat_ref, ybt_ref, wo_hbm, gain_ref, wg_hbm, wu_hbm, wd_hbm, fgain_ref,
                    o_ref, wo_ref, wg_ref, wu_ref, wd_ref, stage_cols, stage_rows, sems,
                    *, layer, final_norm):
    first_step = pl.program_id(0) == 0
    weights = _FfnWeights(layer, (wg_hbm, wu_hbm, wd_hbm), (wg_ref, wu_ref, wd_ref),
                          stage_cols, stage_rows, sems)

    def stage_out_projection():
        halves = [pltpu.make_async_copy(wo_hbm.at[layer, pl.ds(s * SGU_WIDTH, SGU_WIDTH), :],
                                        stage_rows.at[s], sems.at[s, 2]) for s in range(2)]
        for copy in halves + weights.copies(0, (0, 1)) + weights.copies(1, (0, 1)):
            copy.start()
        for s, copy in enumerate(halves):
            copy.wait()
            wo_ref[s * SGU_WIDTH:(s + 1) * SGU_WIDTH, :] = stage_rows[s].astype(bf16)
        for copy in weights.copies(0, (2,)) + weights.copies(1, (2,)):
            copy.start()

    def step(landing):
        if landing:
            stage_out_projection()
        x = x_ref[...]
        x = (x + _dot_tn(yat_ref[0], wo_ref[:SGU_WIDTH, :])
             + _dot_tn(ybt_ref[0], wo_ref[SGU_WIDTH:, :]))
        x = _swiglu_half_step(x, gain_ref[...], weights, landing)
        if final_norm:
            x = _rmsnorm_rows(x, fgain_ref[...])
        o_ref[...] = x

    pl.when(first_step)(functools.partial(step, True))
    pl.when(jnp.logical_not(first_step))(functools.partial(step, False))


def _const_spec(shape):
    nd = len(shape)
    return pl.BlockSpec(shape, lambda *_: (0,) * nd, pipeline_mode=pl.Buffered(1))


def _layer_spec(layer, shape):
    nd = len(shape)
    return pl.BlockSpec((None,) + tuple(shape), lambda *_: (layer,) + (0,) * nd,
                        pipeline_mode=pl.Buffered(1))


_HBM_SPEC = pl.BlockSpec(memory_space=pl.ANY)


def _ffn_weight_scratch():
    return [pltpu.VMEM((D_MODEL, D_FF), bf16), pltpu.VMEM((D_MODEL, D_FF), bf16),
            pltpu.VMEM((D_FF, D_MODEL), bf16),
            pltpu.VMEM((2, 2, D_MODEL, FFN_CHUNK_MAX), f32),
            pltpu.VMEM((2, FFN_CHUNK_MAX, D_MODEL), f32),
            pltpu.SemaphoreType.DMA((2, 3))]


def _ffn_call(layer, x2d, gain, wg, wu, wd):
    t = x2d.shape[0]
    tm = FFN_TOKENS
    row_spec = pl.BlockSpec((tm, D_MODEL), lambda i: (i, 0))
    return pl.pallas_call(
        functools.partial(_ffn_kernel, layer=layer),
        grid=(t // tm,),
        in_specs=[row_spec, _layer_spec(layer, (1, D_MODEL)), _HBM_SPEC, _HBM_SPEC, _HBM_SPEC],
        out_specs=row_spec,
        out_shape=jax.ShapeDtypeStruct((t, D_MODEL), f32),
        scratch_shapes=_ffn_weight_scratch(),
        compiler_params=pltpu.CompilerParams(
            dimension_semantics=("arbitrary",), vmem_limit_bytes=VMEM_LIMIT_BYTES),
        name="ffn",
    )(x2d, gain, wg, wu, wd)


def _mix_ffn_call(layer, x2d, yat, ybt, wo, gain, wg, wu, wd, fgain, final_norm):
    t = x2d.shape[0]
    tm = FFN_TOKENS
    tiles_per_seq = ybt.shape[2] // tm
    row_spec = pl.BlockSpec((tm, D_MODEL), lambda i: (i, 0))
    cm_spec = pl.BlockSpec((1, SGU_WIDTH, tm),
                           lambda i: (i // tiles_per_seq, 0, i % tiles_per_seq))
    return pl.pallas_call(
        functools.partial(_mix_ffn_kernel, layer=layer, final_norm=final_norm),
        grid=(t // tm,),
        in_specs=[row_spec, cm_spec, cm_spec, _HBM_SPEC, _layer_spec(layer, (1, D_MODEL)),
                  _HBM_SPEC, _HBM_SPEC, _HBM_SPEC, _const_spec((1, D_MODEL))],
        out_specs=row_spec,
        out_shape=jax.ShapeDtypeStruct((t, D_MODEL), f32),
        scratch_shapes=[pltpu.VMEM((2 * SGU_WIDTH, D_MODEL), bf16)] + _ffn_weight_scratch(),
        compiler_params=pltpu.CompilerParams(
            dimension_semantics=("arbitrary",), vmem_limit_bytes=VMEM_LIMIT_BYTES),
        name="mix_ffn",
    )(x2d, yat, ybt, wo, gain, wg, wu, wd, fgain)


def _gelu_exact(x):
    return 0.5 * x * (1.0 + lax.erf(x * (1.0 / math.sqrt(2.0))))


_COL_ZV, _COL_ZU, _COL_Q, _COL_K, _COL_V = (
    SGU_WIDTH, 0, 2 * SGU_WIDTH, 2 * SGU_WIDTH + DIFF_WIDTH, 2 * SGU_WIDTH + 2 * DIFF_WIDTH)
INPROJ_COL_ORDER = (_COL_ZV, _COL_ZU, _COL_Q, _COL_K, _COL_V)
INPROJ_COL_WIDTH = 512
assert SGU_WIDTH == DIFF_WIDTH == INPROJ_COL_WIDTH


def _inproj_kernel(x_ref, gain_ref, win_hbm, sgain_ref, ws_ref, bs_ref,
                   yat_ref, qt_ref, k_ref, vt_ref, win_ref, stage_ref, sems, *, layer):
    tm = x_ref.shape[0]
    width = INPROJ_COL_WIDTH
    first_step = pl.program_id(0) == 0

    def copy(n):
        return pltpu.make_async_copy(
            win_hbm.at[layer, :, pl.ds(INPROJ_COL_ORDER[n], width)],
            stage_ref.at[n % 2], sems.at[n % 2])

    def step(landing):
        def columns(n):
            c0 = INPROJ_COL_ORDER[n]
            if landing:
                copy(n).wait()
                win_ref[:, c0:c0 + width] = stage_ref[n % 2].astype(bf16)
                if n + 2 < len(INPROJ_COL_ORDER):
                    copy(n + 2).start()
            return win_ref[:, c0:c0 + width]

        if landing:
            copy(0).start()
            copy(1).start()
        h = _rmsnorm_rows(x_ref[...], gain_ref[...]).astype(bf16)

        n_chunks = tm // CHUNK
        v = _gelu_exact(_dot_nt_t(columns(0), h))
        zu = _dot_nt_t(columns(1), h)
        stacked = []
        for g in range(SGU_GROUPS):
            lo, hi = g * SGU_GROUP_DIM, (g + 1) * SGU_GROUP_DIM
            vg = v[lo:hi]
            ms = jnp.mean(vg * vg, axis=0, keepdims=True)
            vn = (vg * lax.rsqrt(ms + EPS) * sgain_ref[lo:hi]).astype(bf16)
            stacked.append(jnp.concatenate(
                [vn[:, c * CHUNK:(c + 1) * CHUNK] for c in range(n_chunks)], axis=0))
        qt = _dot_nt_t(columns(2), h)
        u = _gelu_exact(zu)

        row = lax.broadcasted_iota(jnp.int32, (CHUNK, CHUNK), 0)
        col = lax.broadcasted_iota(jnp.int32, (CHUNK, CHUNK), 1)
        causal = col <= row
        gates = []
        for g in range(SGU_GROUPS):
            w = jnp.where(causal, ws_ref[g], 0.0).astype(bf16)
            gates.append(_dot_nt(stacked[g], w) + bs_ref[g])
        k = _dot(h, columns(3))
        qt_ref[0] = (qt * (DIFF_QK_DIM ** -0.5 * LOG2_E)).astype(bf16)
        for g in range(SGU_GROUPS):
            lo, hi = g * SGU_GROUP_DIM, (g + 1) * SGU_GROUP_DIM
            for c in range(n_chunks):
                gate = gates[g][c * SGU_GROUP_DIM:(c + 1) * SGU_GROUP_DIM]
                yat_ref[0, lo:hi, c * CHUNK:(c + 1) * CHUNK] = (
                    u[lo:hi, c * CHUNK:(c + 1) * CHUNK] * gate).astype(bf16)

        vt = _dot_nt_t(columns(4), h)
        k_ref[...] = k.astype(bf16)
        vt = vt.astype(bf16)
        ones_rows = jnp.ones((VT_ROWS - DIFF_V_DIM, ATTN_BLOCK), bf16)
        for hd in range(DIFF_HEADS):
            for c in range(tm // ATTN_BLOCK):
                vt_ref[0, hd, c, :DIFF_V_DIM, :] = vt[hd * DIFF_V_DIM:(hd + 1) * DIFF_V_DIM,
                                                      c * ATTN_BLOCK:(c + 1) * ATTN_BLOCK]
                vt_ref[0, hd, c, DIFF_V_DIM:, :] = ones_rows

    pl.when(first_step)(functools.partial(step, True))
    pl.when(jnp.logical_not(first_step))(functools.partial(step, False))


def _dot_nt_t(w, h):
    return lax.dot_general(w, h, (((0,), (1,)), ((), ())), preferred_element_type=f32)


def _inproj_call(layer, x2d, seq, gain, win, sgain, ws, bs):
    t = x2d.shape[0]
    tm = INPROJ_TOKENS
    tiles_per_seq = seq // tm
    row_spec = pl.BlockSpec((tm, D_MODEL), lambda i: (i, 0))
    half_spec = pl.BlockSpec((tm, SGU_WIDTH), lambda i: (i, 0))
    half_shape = jax.ShapeDtypeStruct((t, SGU_WIDTH), bf16)
    cm_spec = pl.BlockSpec((1, SGU_WIDTH, tm), lambda i: (i // tiles_per_seq, 0, i % tiles_per_seq))
    cm_shape = jax.ShapeDtypeStruct((t // seq, SGU_WIDTH, seq), bf16)
    vt_spec = pl.BlockSpec((1, DIFF_HEADS, tm // ATTN_BLOCK, VT_ROWS, ATTN_BLOCK),
                           lambda i: (i // tiles_per_seq, 0, i % tiles_per_seq, 0, 0))
    vt_shape = jax.ShapeDtypeStruct(
        (t // seq, DIFF_HEADS, seq // ATTN_BLOCK, VT_ROWS, ATTN_BLOCK), bf16)
    return pl.pallas_call(
        functools.partial(_inproj_kernel, layer=layer),
        grid=(t // tm,),
        in_specs=[row_spec, _layer_spec(layer, (1, D_MODEL)), _HBM_SPEC,
                  _layer_spec(layer, (SGU_WIDTH, 1)),
                  _layer_spec(layer, (SGU_GROUPS, CHUNK, CHUNK)),
                  _layer_spec(layer, (SGU_GROUPS, 1, CHUNK))],
        out_specs=[cm_spec, cm_spec, half_spec, vt_spec],
        out_shape=[cm_shape, cm_shape, half_shape, vt_shape],
        scratch_shapes=[pltpu.VMEM((D_MODEL, IN_COLS), bf16),
                        pltpu.VMEM((2, D_MODEL, INPROJ_COL_WIDTH), f32),
                        pltpu.SemaphoreType.DMA((2,))],
        compiler_params=pltpu.CompilerParams(
            dimension_semantics=("arbitrary",), vmem_limit_bytes=VMEM_LIMIT_BYTES),
        name="inproj_sgu",
    )(x2d, gain, win, sgain, ws, bs)


def _attn_kernel(lq1_ref, lk1_ref, lq2_ref, lk2_ref, gain_ref, kbias_ref, qt_ref, k_ref, vt_ref,
                 o_ref, qop_ref, m_ref, acc_ref, *, lam_init):
    blk = ATTN_BLOCK
    pair = pl.program_id(1)
    heads = range(DIFF_HEADS)
    slopes = [2.0 ** (-8.0 * (hd + 1) / DIFF_HEADS) * LOG2_E for hd in heads]
    krow = lax.broadcasted_iota(jnp.int32, (blk, 2 * blk), 0)
    qcol = lax.broadcasted_iota(jnp.int32, (blk, 2 * blk), 1) & (blk - 1)
    causal = krow <= qcol

    row = lax.broadcasted_iota(jnp.int32, (DIFF_V_DIM, 2 * blk), 0)
    c1 = jnp.asarray(LOG2_E, f32).astype(bf16).astype(f32)
    q_tail = jnp.where(row == 0, c1, jnp.where(row == 1, LOG2_E - c1, 0.0)).astype(bf16)
    q_zero = jnp.zeros((DIFF_QK_DIM, blk), bf16)
    for qs in range(2):
        for hd in heads:
            qt = qt_ref[0, hd * DIFF_V_DIM:(hd + 1) * DIFF_V_DIM, qs * blk:(qs + 1) * blk]
            qop_ref[qs, hd, :DIFF_QK_DIM, :blk] = qt[:DIFF_QK_DIM]
            qop_ref[qs, hd, :DIFF_QK_DIM, blk:] = q_zero
            qop_ref[qs, hd, DIFF_QK_DIM:DIFF_V_DIM, :blk] = q_zero
            qop_ref[qs, hd, DIFF_QK_DIM:DIFF_V_DIM, blk:] = qt[DIFF_QK_DIM:]
            qop_ref[qs, hd, DIFF_V_DIM:, :] = q_tail

    def scores(item):
        qs, kb, hd, diagonal = item
        k_rows = k_ref[0, pl.ds(pl.multiple_of(kb * blk, blk), blk),
                       hd * DIFF_V_DIM:(hd + 1) * DIFF_V_DIM]
        s = _dot(jnp.concatenate([k_rows, kbias_ref[hd]], axis=1), qop_ref[qs, hd])
        if diagonal:
            s = jnp.where(causal, s, NEG_INF)
        return s

    def softmax(item, s):
        qs, kb, hd, diagonal = item
        s_max = jnp.max(s, axis=0, keepdims=True)
        if diagonal:
            m_new = s_max
            p, alpha = jnp.exp2(s - m_new), None
        else:
            beta = slopes[hd] * ((kb - (2 * pair + qs)) * blk).astype(f32)
            m_old = m_ref[qs, hd]
            m_new = jnp.maximum(m_old, s_max + beta)
            p, alpha = jnp.exp2(s - (m_new - beta)), jnp.exp2(m_old - m_new)
        m_ref[qs, hd] = m_new
        return item, p.astype(bf16), alpha

    def accumulate(item, p, alpha):
        qs, kb, hd, _ = item
        pv = _dot(vt_ref[0, hd, kb], p)
        acc_ref[qs, hd] = pv if alpha is None else alpha * acc_ref[qs, hd] + pv

    def run(items):
        scored = [scores(item) for item in items[:ATTN_LOOKAHEAD]]
        weighted = []
        for n, item in enumerate(items):
            if n + ATTN_LOOKAHEAD < len(items):
                scored.append(scores(items[n + ATTN_LOOKAHEAD]))
            weighted.append(softmax(item, scored.pop(0)))
            if len(weighted) > ATTN_PV_DELAY:
                accumulate(*weighted.pop(0))
        for args in weighted:
            accumulate(*args)

    first, second = 2 * pair, 2 * pair + 1
    run([(0, first, hd, True) for hd in heads] + [(1, second, hd, True) for hd in heads]
        + [(1, first, hd, False) for hd in heads])

    def earlier_keys(j, carry):
        run([(qs, 2 * j + kbo, hd, False) for kbo in range(2) for qs in range(2) for hd in heads])
        return carry

    lax.fori_loop(0, pair, earlier_keys, 0)

    lam = (jnp.exp(jnp.sum(lq1_ref[...] * lk1_ref[...], axis=-1, keepdims=True))
           - jnp.exp(jnp.sum(lq2_ref[...] * lk2_ref[...], axis=-1, keepdims=True))
           + lam_init)
    gain = gain_ref[...] * (1.0 - lam_init)
    for qs in range(2):
        for hd in heads:
            acc1, acc2 = acc_ref[qs, hd, :, :blk], acc_ref[qs, hd, :, blk:]
            o = (acc1[:DIFF_V_DIM] / acc1[DIFF_V_DIM:DIFF_V_DIM + 1]
                 - lam * (acc2[:DIFF_V_DIM] / acc2[DIFF_V_DIM:DIFF_V_DIM + 1]))
            ms = jnp.mean(o * o, axis=0, keepdims=True)
            o_ref[0, hd * DIFF_V_DIM:(hd + 1) * DIFF_V_DIM, qs * blk:(qs + 1) * blk] = (
                o * lax.rsqrt(ms + EPS) * gain).astype(bf16)


def _attn_call(layer, lq1, lk1, lq2, lk2, gain, kbias, qt, k, vt, lam_init):
    b, s, _ = k.shape
    blk = ATTN_BLOCK
    lam_spec = _layer_spec(layer, (1, DIFF_QK_DIM))
    q_spec = pl.BlockSpec((1, DIFF_WIDTH, 2 * blk), lambda bi, pi: (bi, 0, pi))
    k_spec = pl.BlockSpec((1, s, DIFF_WIDTH), lambda bi, pi: (bi, 0, 0))
    vt_spec = pl.BlockSpec((1, DIFF_HEADS, s // blk, VT_ROWS, blk),
                           lambda bi, pi: (bi, 0, 0, 0, 0))
    return pl.pallas_call(
        functools.partial(_attn_kernel, lam_init=lam_init),
        grid=(b, s // (2 * blk)),
        in_specs=[lam_spec, lam_spec, lam_spec, lam_spec, _layer_spec(layer, (DIFF_V_DIM, 1)),
                  _const_spec((DIFF_HEADS, blk, DIFF_V_DIM)), q_spec, k_spec, vt_spec],
        out_specs=pl.BlockSpec((1, DIFF_WIDTH, 2 * blk), lambda bi, pi: (bi, 0, pi)),
        out_shape=jax.ShapeDtypeStruct((b, DIFF_WIDTH, s), bf16),
        scratch_shapes=[pltpu.VMEM((2, DIFF_HEADS, 2 * DIFF_V_DIM, 2 * blk), bf16),
                        pltpu.VMEM((2, DIFF_HEADS, 1, 2 * blk), f32),
                        pltpu.VMEM((2, DIFF_HEADS, VT_ROWS, 2 * blk), f32)],
        compiler_params=pltpu.CompilerParams(
            dimension_semantics=("parallel", "arbitrary"),
            vmem_limit_bytes=VMEM_LIMIT_BYTES),
        name="diff_attn",
    )(lq1, lk1, lq2, lk2, gain, kbias, qt, k, vt)


def _alibi_key_columns():
    r = jnp.arange(ATTN_BLOCK, dtype=f32)
    slopes = jnp.exp2(-8.0 * jnp.arange(1, DIFF_HEADS + 1, dtype=f32) / DIFF_HEADS)
    cols = jnp.zeros((DIFF_HEADS, ATTN_BLOCK, DIFF_V_DIM), f32)
    cols = cols.at[:, :, 0:2].set((slopes[:, None] * r[None, :])[:, :, None])
    return cols.astype(bf16)


def kernel(x, ffn1_norm, ffn1_w_gate, ffn1_w_up, ffn1_w_down, mix_norm, w_in, sgu_norm, sgu_w, sgu_b, lambda_q1, lambda_k1, lambda_q2, lambda_k2, diff_subln, w_out, ffn2_norm, ffn2_w_gate, ffn2_w_up, ffn2_w_down, final_norm):
    b, s, d = x.shape
    depth = w_in.shape[0]
    t = b * s
    x2d = x.reshape(t, d)
    kbias = _alibi_key_columns()

    as_rows = lambda p: p.reshape(depth, 1, -1)
    wg1, wu1, wd1 = ffn1_w_gate, ffn1_w_up, ffn1_w_down
    wg2, wu2, wd2 = ffn2_w_gate, ffn2_w_up, ffn2_w_down
    win = w_in
    wo = w_out
    sgu_norm_cols = sgu_norm.reshape(depth, SGU_WIDTH, 1)
    sgu_b_rows = sgu_b.reshape(depth, SGU_GROUPS, 1, CHUNK)
    subln_cols = diff_subln.reshape(depth, DIFF_V_DIM, 1)

    for l in range(depth):
        x2d = _ffn_call(l, x2d, as_rows(ffn1_norm), wg1, wu1, wd1)
        yat, qt, k, vt = _inproj_call(l, x2d, s, as_rows(mix_norm), win,
                                      sgu_norm_cols, sgu_w, sgu_b_rows)
        lam_init = 0.8 - 0.6 * math.exp(-0.3 * l)
        ybt = _attn_call(l, as_rows(lambda_q1), as_rows(lambda_k1), as_rows(lambda_q2),
                         as_rows(lambda_k2), subln_cols, kbias,
                         qt, k.reshape(b, s, DIFF_WIDTH), vt, lam_init)
        x2d = _mix_ffn_call(l, x2d, yat, ybt, wo, as_rows(ffn2_norm),
                            wg2, wu2, wd2, final_norm.reshape(1, d),
                            final_norm=(l == depth - 1))
    return x2d.reshape(b, s, d)
```

```python
import functools
import math

import jax
import jax.numpy as jnp
from jax import lax
from jax.experimental import pallas as pl
from jax.experimental.pallas import tpu as pltpu

D_MODEL = 1024
D_FF = 2816
SGU_WIDTH = 512
SGU_GROUP_DIM = 64
SGU_GROUPS = SGU_WIDTH // SGU_GROUP_DIM
CHUNK = 128
DIFF_WIDTH = 512
DIFF_V_DIM = 128
DIFF_QK_DIM = 64
DIFF_HEADS = DIFF_WIDTH // DIFF_V_DIM
IN_COLS = 2 * SGU_WIDTH + 3 * DIFF_WIDTH
EPS = 1e-6
NEG_INF = -1e30
LOG2_E = math.log2(math.e)

VMEM_LIMIT_BYTES = 56 * 1024 * 1024

FFN_TOKENS = 512
FFN_CHUNKS = ((0, 512), (512, 1024), (1024, 1536), (1536, 2048), (2048, 2560), (2560, 2816))
FFN_CHUNK_MAX = max(c1 - c0 for c0, c1 in FFN_CHUNKS)
assert FFN_CHUNK_MAX == SGU_WIDTH == DIFF_WIDTH
INPROJ_TOKENS = 1024
ATTN_BLOCK = 256
ATTN_GROUP = 4
VT_ROWS = DIFF_V_DIM + 16
ATTN_LOOKAHEAD = 3
ATTN_PV_DELAY = 2

bf16 = jnp.bfloat16
f32 = jnp.float32


def _rmsnorm_rows(x, gain):
    ms = jnp.mean(x * x, axis=-1, keepdims=True)
    return x * lax.rsqrt(ms + EPS) * gain


def _dot(a, b):
    return jnp.dot(a, b, preferred_element_type=f32)


def _dot_nt(a, b):
    return lax.dot_general(a, b, (((1,), (1,)), ((), ())), preferred_element_type=f32)


def _dot_tn(a, b):
    return lax.dot_general(a, b, (((0,), (0,)), ((), ())), preferred_element_type=f32)


class _FfnWeights:
    def __init__(self, layer, hbm, resident, stage_cols, stage_rows, sems):
        self.layer = layer
        self.wg_hbm, self.wu_hbm, self.wd_hbm = hbm
        self.wg, self.wu, self.wd = resident
        self.stage_cols, self.stage_rows, self.sems = stage_cols, stage_rows, sems

    def copies(self, c, kinds=(0, 1, 2)):
        c0, c1 = FFN_CHUNKS[c]
        w, slot = c1 - c0, c % 2
        endpoints = (
            (self.wg_hbm.at[self.layer, :, pl.ds(c0, w)], self.stage_cols.at[slot, 0, :, pl.ds(0, w)]),
            (self.wu_hbm.at[self.layer, :, pl.ds(c0, w)], self.stage_cols.at[slot, 1, :, pl.ds(0, w)]),
            (self.wd_hbm.at[self.layer, pl.ds(c0, w), :], self.stage_rows.at[slot, pl.ds(0, w), :]))
        return [pltpu.make_async_copy(*endpoints[k], self.sems.at[slot, k]) for k in kinds]

    def land(self, c):
        c0, c1 = FFN_CHUNKS[c]
        w, slot = c1 - c0, c % 2
        for copy in self.copies(c):
            copy.wait()
        self.wg[:, c0:c1] = self.stage_cols[slot, 0, :, :w].astype(bf16)
        self.wu[:, c0:c1] = self.stage_cols[slot, 1, :, :w].astype(bf16)
        self.wd[c0:c1, :] = self.stage_rows[slot, :w, :].astype(bf16)
        if c + 2 < len(FFN_CHUNKS):
            for copy in self.copies(c + 2):
                copy.start()


def _swiglu_half_step(x, gain, weights, landing):
    h = _rmsnorm_rows(x, gain).astype(bf16)
    acc = None
    for c, (c0, c1) in enumerate(FFN_CHUNKS):
        if landing:
            weights.land(c)
        g = _dot(h, weights.wg[:, c0:c1])
        u = _dot(h, weights.wu[:, c0:c1])
        a = (g / (1.0 + jnp.exp(-g)) * u).astype(bf16)
        d = _dot(a, weights.wd[c0:c1, :])
        acc = d if acc is None else acc + d
    return x + 0.5 * acc


def _ffn_kernel(x_ref, gain_ref, wg_hbm, wu_hbm, wd_hbm, o_ref,
                wg_ref, wu_ref, wd_ref, stage_cols, stage_rows, sems, *, layer):
    first_step = pl.program_id(0) == 0
    weights = _FfnWeights(layer, (wg_hbm, wu_hbm, wd_hbm), (wg_ref, wu_ref, wd_ref),
                          stage_cols, stage_rows, sems)

    @pl.when(first_step)
    def _():
        for copy in weights.copies(0) + weights.copies(1):
            copy.start()
        o_ref[...] = _swiglu_half_step(x_ref[...], gain_ref[...], weights, landing=True)

    @pl.when(jnp.logical_not(first_step))
    def _():
        o_ref[...] = _swiglu_half_step(x_ref[...], gain_ref[...], weights, landing=False)


def _mix_ffn_kernel(x_ref, yat_ref, ybt_ref, wo_hbm, gain_ref, wg_hbm, wu_hbm, wd_hbm, fgain_ref,
                    o_ref, wo_ref, wg_ref, wu_ref, wd_ref, stage_cols, stage_rows, sems,
                    *, layer, final_norm):
    first_step = pl.program_id(0) == 0
    weights = _FfnWeights(layer, (wg_hbm, wu_hbm, wd_hbm), (wg_ref, wu_ref, wd_ref),
                          stage_cols, stage_rows, sems)

    def stage_out_projection():
        halves = [pltpu.make_async_copy(wo_hbm.at[layer, pl.ds(s * SGU_WIDTH, SGU_WIDTH), :],
                                        stage_rows.at[s], sems.at[s, 2]) for s in range(2)]
        for copy in halves + weights.copies(0, (0, 1)) + weights.copies(1, (0, 1)):
            copy.start()
        for s, copy in enumerate(halves):
            copy.wait()
            wo_ref[s * SGU_WIDTH:(s + 1) * SGU_WIDTH, :] = stage_rows[s].astype(bf16)
        for copy in weights.copies(0, (2,)) + weights.copies(1, (2,)):
            copy.start()

    def step(landing):
        if landing:
            stage_out_projection()
        x = x_ref[...]
        x = (x + _dot_tn(yat_ref[0], wo_ref[:SGU_WIDTH, :])
             + _dot_tn(ybt_ref[0], wo_ref[SGU_WIDTH:, :]))
        x = _swiglu_half_step(x, gain_ref[...], weights, landing)
        if final_norm:
            x = _rmsnorm_rows(x, fgain_ref[...])
        o_ref[...] = x

    pl.when(first_step)(functools.partial(step, True))
    pl.when(jnp.logical_not(first_step))(functools.partial(step, False))


def _const_spec(shape):
    nd = len(shape)
    return pl.BlockSpec(shape, lambda *_: (0,) * nd, pipeline_mode=pl.Buffered(1))


def _layer_spec(layer, shape):
    nd = len(shape)
    return pl.BlockSpec((None,) + tuple(shape), lambda *_: (layer,) + (0,) * nd,
                        pipeline_mode=pl.Buffered(1))


_HBM_SPEC = pl.BlockSpec(memory_space=pl.ANY)


def _ffn_weight_scratch():
    return [pltpu.VMEM((D_MODEL, D_FF), bf16), pltpu.VMEM((D_MODEL, D_FF), bf16),
            pltpu.VMEM((D_FF, D_MODEL), bf16),
            pltpu.VMEM((2, 2, D_MODEL, FFN_CHUNK_MAX), f32),
            pltpu.VMEM((2, FFN_CHUNK_MAX, D_MODEL), f32),
            pltpu.SemaphoreType.DMA((2, 3))]


def _ffn_call(layer, x2d, gain, wg, wu, wd):
    t = x2d.shape[0]
    tm = FFN_TOKENS
    row_spec = pl.BlockSpec((tm, D_MODEL), lambda i: (i, 0))
    return pl.pallas_call(
        functools.partial(_ffn_kernel, layer=layer),
        grid=(t // tm,),
        in_specs=[row_spec, _layer_spec(layer, (1, D_MODEL)), _HBM_SPEC, _HBM_SPEC, _HBM_SPEC],
        out_specs=row_spec,
        out_shape=jax.ShapeDtypeStruct((t, D_MODEL), f32),
        scratch_shapes=_ffn_weight_scratch(),
        compiler_params=pltpu.CompilerParams(
            dimension_semantics=("arbitrary",), vmem_limit_bytes=VMEM_LIMIT_BYTES),
        name="ffn",
    )(x2d, gain, wg, wu, wd)


def _mix_ffn_call(layer, x2d, yat, ybt, wo, gain, wg, wu, wd, fgain, final_norm):
    t = x2d.shape[0]
    tm = FFN_TOKENS
    tiles_per_seq = ybt.shape[2] // tm
    row_spec = pl.BlockSpec((tm, D_MODEL), lambda i: (i, 0))
    cm_spec = pl.BlockSpec((1, SGU_WIDTH, tm),
                           lambda i: (i // tiles_per_seq, 0, i % tiles_per_seq))
    return pl.pallas_call(
        functools.partial(_mix_ffn_kernel, layer=layer, final_norm=final_norm),
        grid=(t // tm,),
        in_specs=[row_spec, cm_spec, cm_spec, _HBM_SPEC, _layer_spec(layer, (1, D_MODEL)),
                  _HBM_SPEC, _HBM_SPEC, _HBM_SPEC, _const_spec((1, D_MODEL))],
        out_specs=row_spec,
        out_shape=jax.ShapeDtypeStruct((t, D_MODEL), f32),
        scratch_shapes=[pltpu.VMEM((2 * SGU_WIDTH, D_MODEL), bf16)] + _ffn_weight_scratch(),
        compiler_params=pltpu.CompilerParams(
            dimension_semantics=("arbitrary",), vmem_limit_bytes=VMEM_LIMIT_BYTES),
        name="mix_ffn",
    )(x2d, yat, ybt, wo, gain, wg, wu, wd, fgain)


def _gelu_exact(x):
    return 0.5 * x * (1.0 + lax.erf(x * (1.0 / math.sqrt(2.0))))


_COL_ZV, _COL_ZU, _COL_Q, _COL_K, _COL_V = (
    SGU_WIDTH, 0, 2 * SGU_WIDTH, 2 * SGU_WIDTH + DIFF_WIDTH, 2 * SGU_WIDTH + 2 * DIFF_WIDTH)
INPROJ_COL_ORDER = (_COL_ZV, _COL_ZU, _COL_Q, _COL_K, _COL_V)
INPROJ_COL_WIDTH = 512
assert SGU_WIDTH == DIFF_WIDTH == INPROJ_COL_WIDTH


def _inproj_kernel(x_ref, gain_ref, win_hbm, sgain_ref, ws_ref, bs_ref,
                   yat_ref, qt_ref, k_ref, vt_ref, win_ref, stage_ref, sems, *, layer):
    tm = x_ref.shape[0]
    width = INPROJ_COL_WIDTH
    first_step = pl.program_id(0) == 0

    def copy(n):
        return pltpu.make_async_copy(
            win_hbm.at[layer, :, pl.ds(INPROJ_COL_ORDER[n], width)],
            stage_ref.at[n % 2], sems.at[n % 2])

    def step(landing):
        def columns(n):
            c0 = INPROJ_COL_ORDER[n]
            if landing:
                copy(n).wait()
                win_ref[:, c0:c0 + width] = stage_ref[n % 2].astype(bf16)
                if n + 2 < len(INPROJ_COL_ORDER):
                    copy(n + 2).start()
            return win_ref[:, c0:c0 + width]

        if landing:
            copy(0).start()
            copy(1).start()
        h = _rmsnorm_rows(x_ref[...], gain_ref[...]).astype(bf16)

        n_chunks = tm // CHUNK
        v = _gelu_exact(_dot_nt_t(columns(0), h))
        zu = _dot_nt_t(columns(1), h)
        stacked = []
        for g in range(SGU_GROUPS):
            lo, hi = g * SGU_GROUP_DIM, (g + 1) * SGU_GROUP_DIM
            vg = v[lo:hi]
            ms = jnp.mean(vg * vg, axis=0, keepdims=True)
            vn = (vg * lax.rsqrt(ms + EPS) * sgain_ref[lo:hi]).astype(bf16)
            stacked.append(jnp.concatenate(
                [vn[:, c * CHUNK:(c + 1) * CHUNK] for c in range(n_chunks)], axis=0))
        qt = _dot_nt_t(columns(2), h)
        u = _gelu_exact(zu)

        row = lax.broadcasted_iota(jnp.int32, (CHUNK, CHUNK), 0)
        col = lax.broadcasted_iota(jnp.int32, (CHUNK, CHUNK), 1)
        causal = col <= row
        gates = []
        for g in range(SGU_GROUPS):
            w = jnp.where(causal, ws_ref[g], 0.0).astype(bf16)
            gates.append(_dot_nt(stacked[g], w) + bs_ref[g])
        k = _dot(h, columns(3))
        qt_ref[0] = (qt * (DIFF_QK_DIM ** -0.5 * LOG2_E)).astype(bf16)
        for g in range(SGU_GROUPS):
            lo, hi = g * SGU_GROUP_DIM, (g + 1) * SGU_GROUP_DIM
            for c in range(n_chunks):
                gate = gates[g][c * SGU_GROUP_DIM:(c + 1) * SGU_GROUP_DIM]
                yat_ref[0, lo:hi, c * CHUNK:(c + 1) * CHUNK] = (
                    u[lo:hi, c * CHUNK:(c + 1) * CHUNK] * gate).astype(bf16)

        vt = _dot_nt_t(columns(4), h)
        k_ref[...] = k.astype(bf16)
        vt = vt.astype(bf16)
        ones_rows = jnp.ones((VT_ROWS - DIFF_V_DIM, ATTN_BLOCK), bf16)
        for hd in range(DIFF_HEADS):
            for c in range(tm // ATTN_BLOCK):
                vt_ref[0, hd, c, :DIFF_V_DIM, :] = vt[hd * DIFF_V_DIM:(hd + 1) * DIFF_V_DIM,
                                                      c * ATTN_BLOCK:(c + 1) * ATTN_BLOCK]
                vt_ref[0, hd, c, DIFF_V_DIM:, :] = ones_rows

    pl.when(first_step)(functools.partial(step, True))
    pl.when(jnp.logical_not(first_step))(functools.partial(step, False))


def _dot_nt_t(w, h):
    return lax.dot_general(w, h, (((0,), (1,)), ((), ())), preferred_element_type=f32)


def _inproj_call(layer, x2d, seq, gain, win, sgain, ws, bs):
    t = x2d.shape[0]
    tm = INPROJ_TOKENS
    tiles_per_seq = seq // tm
    row_spec = pl.BlockSpec((tm, D_MODEL), lambda i: (i, 0))
    half_spec = pl.BlockSpec((tm, SGU_WIDTH), lambda i: (i, 0))
    half_shape = jax.ShapeDtypeStruct((t, SGU_WIDTH), bf16)
    cm_spec = pl.BlockSpec((1, SGU_WIDTH, tm), lambda i: (i // tiles_per_seq, 0, i % tiles_per_seq))
    cm_shape = jax.ShapeDtypeStruct((t // seq, SGU_WIDTH, seq), bf16)
    vt_spec = pl.BlockSpec((1, DIFF_HEADS, tm // ATTN_BLOCK, VT_ROWS, ATTN_BLOCK),
                           lambda i: (i // tiles_per_seq, 0, i % tiles_per_seq, 0, 0))
    vt_shape = jax.ShapeDtypeStruct(
        (t // seq, DIFF_HEADS, seq // ATTN_BLOCK, VT_ROWS, ATTN_BLOCK), bf16)
    return pl.pallas_call(
        functools.partial(_inproj_kernel, layer=layer),
        grid=(t // tm,),
        in_specs=[row_spec, _layer_spec(layer, (1, D_MODEL)), _HBM_SPEC,
                  _layer_spec(layer, (SGU_WIDTH, 1)),
                  _layer_spec(layer, (SGU_GROUPS, CHUNK, CHUNK)),
                  _layer_spec(layer, (SGU_GROUPS, 1, CHUNK))],
        out_specs=[cm_spec, cm_spec, half_spec, vt_spec],
        out_shape=[cm_shape, cm_shape, half_shape, vt_shape],
        scratch_shapes=[pltpu.VMEM((D_MODEL, IN_COLS), bf16),
                        pltpu.VMEM((2, D_MODEL, INPROJ_COL_WIDTH), f32),
                        pltpu.SemaphoreType.DMA((2,))],
        compiler_params=pltpu.CompilerParams(
            dimension_semantics=("arbitrary",), vmem_limit_bytes=VMEM_LIMIT_BYTES),
        name="inproj_sgu",
    )(x2d, gain, win, sgain, ws, bs)


def _attn_kernel(lq1_ref, lk1_ref, lq2_ref, lk2_ref, gain_ref, kbias_ref, qt_ref, k_ref, vt_ref,
                 o_ref, qop_ref, m_ref, acc_ref, *, lam_init):
    blk = ATTN_BLOCK
    group = pl.program_id(1)
    first_block = group * ATTN_GROUP
    heads = range(DIFF_HEADS)
    slopes = [2.0 ** (-8.0 * (hd + 1) / DIFF_HEADS) * LOG2_E for hd in heads]
    krow = lax.broadcasted_iota(jnp.int32, (blk, 2 * blk), 0)
    qcol = lax.broadcasted_iota(jnp.int32, (blk, 2 * blk), 1) & (blk - 1)
    causal = krow <= qcol

    row = lax.broadcasted_iota(jnp.int32, (DIFF_V_DIM, 2 * blk), 0)
    c1 = jnp.asarray(LOG2_E, f32).astype(bf16).astype(f32)
    q_tail = jnp.where(row == 0, c1, jnp.where(row == 1, LOG2_E - c1, 0.0)).astype(bf16)
    q_zero = jnp.zeros((DIFF_QK_DIM, blk), bf16)
    for qs in range(ATTN_GROUP):
        for hd in heads:
            qt = qt_ref[0, hd * DIFF_V_DIM:(hd + 1) * DIFF_V_DIM, qs * blk:(qs + 1) * blk]
            qop_ref[qs, hd, :DIFF_QK_DIM, :blk] = qt[:DIFF_QK_DIM]
            qop_ref[qs, hd, :DIFF_QK_DIM, blk:] = q_zero
            qop_ref[qs, hd, DIFF_QK_DIM:DIFF_V_DIM, :blk] = q_zero
            qop_ref[qs, hd, DIFF_QK_DIM:DIFF_V_DIM, blk:] = qt[DIFF_QK_DIM:]
            qop_ref[qs, hd, DIFF_V_DIM:, :] = q_tail

    def scores(item):
        qs, kb, hd, diagonal = item
        k_rows = k_ref[0, pl.ds(pl.multiple_of(kb * blk, blk), blk),
                       hd * DIFF_V_DIM:(hd + 1) * DIFF_V_DIM]
        s = _dot(jnp.concatenate([k_rows, kbias_ref[hd]], axis=1), qop_ref[qs, hd])
        if diagonal:
            s = jnp.where(causal, s, NEG_INF)
        return s

    def softmax(item, s):
        qs, kb, hd, diagonal = item
        s_max = jnp.max(s, axis=0, keepdims=True)
        if diagonal:
            m_new = s_max
            p, alpha = jnp.exp2(s - m_new), None
        else:
            beta = slopes[hd] * ((kb - (first_block + qs)) * blk).astype(f32)
            m_old = m_ref[qs, hd]
            m_new = jnp.maximum(m_old, s_max + beta)
            p, alpha = jnp.exp2(s - (m_new - beta)), jnp.exp2(m_old - m_new)
        m_ref[qs, hd] = m_new
        return item, p.astype(bf16), alpha

    def accumulate(item, p, alpha):
        qs, kb, hd, _ = item
        pv = _dot(vt_ref[0, hd, kb], p)
        acc_ref[qs, hd] = pv if alpha is None else alpha * acc_ref[qs, hd] + pv

    def run(items):
        scored = [scores(item) for item in items[:ATTN_LOOKAHEAD]]
        weighted = []
        for n, item in enumerate(items):
            if n + ATTN_LOOKAHEAD < len(items):
                scored.append(scores(items[n + ATTN_LOOKAHEAD]))
            weighted.append(softmax(item, scored.pop(0)))
            if len(weighted) > ATTN_PV_DELAY:
                accumulate(*weighted.pop(0))
        for args in weighted:
            accumulate(*args)

    diagonal_region = []
    for rnd in range(ATTN_GROUP):
        for qs in range(rnd, ATTN_GROUP):
            key = qs if rnd == 0 else rnd - 1
            diagonal_region += [(qs, first_block + key, hd, rnd == 0) for hd in heads]
    run(diagonal_region)

    def earlier_keys(j, carry):
        run([(qs, 2 * j + kbo, hd, False)
             for kbo in range(2) for qs in range(ATTN_GROUP) for hd in heads])
        return carry

    lax.fori_loop(0, first_block // 2, earlier_keys, 0)

    lam = (jnp.exp(jnp.sum(lq1_ref[...] * lk1_ref[...], axis=-1, keepdims=True))
           - jnp.exp(jnp.sum(lq2_ref[...] * lk2_ref[...], axis=-1, keepdims=True))
           + lam_init)
    gain = gain_ref[...] * (1.0 - lam_init)
    for qs in range(ATTN_GROUP):
        for hd in heads:
            acc1, acc2 = acc_ref[qs, hd, :, :blk], acc_ref[qs, hd, :, blk:]
            o = (acc1[:DIFF_V_DIM] / acc1[DIFF_V_DIM:DIFF_V_DIM + 1]
                 - lam * (acc2[:DIFF_V_DIM] / acc2[DIFF_V_DIM:DIFF_V_DIM + 1]))
            ms = jnp.mean(o * o, axis=0, keepdims=True)
            o_ref[0, hd * DIFF_V_DIM:(hd + 1) * DIFF_V_DIM, qs * blk:(qs + 1) * blk] = (
                o * lax.rsqrt(ms + EPS) * gain).astype(bf16)


def _attn_call(layer, lq1, lk1, lq2, lk2, gain, kbias, qt, k, vt, lam_init):
    b, s, _ = k.shape
    blk = ATTN_BLOCK
    lam_spec = _layer_spec(layer, (1, DIFF_QK_DIM))
    tq = ATTN_GROUP * blk
    q_spec = pl.BlockSpec((1, DIFF_WIDTH, tq), lambda bi, gi: (bi, 0, gi))
    k_spec = pl.BlockSpec((1, s, DIFF_WIDTH), lambda bi, gi: (bi, 0, 0))
    vt_spec = pl.BlockSpec((1, DIFF_HEADS, s // blk, VT_ROWS, blk),
                           lambda bi, gi: (bi, 0, 0, 0, 0))
    return pl.pallas_call(
        functools.partial(_attn_kernel, lam_init=lam_init),
        grid=(b, s // tq),
        in_specs=[lam_spec, lam_spec, lam_spec, lam_spec, _layer_spec(layer, (DIFF_V_DIM, 1)),
                  _const_spec((DIFF_HEADS, blk, DIFF_V_DIM)), q_spec, k_spec, vt_spec],
        out_specs=pl.BlockSpec((1, DIFF_WIDTH, tq), lambda bi, gi: (bi, 0, gi)),
        out_shape=jax.ShapeDtypeStruct((b, DIFF_WIDTH, s), bf16),
        scratch_shapes=[pltpu.VMEM((ATTN_GROUP, DIFF_HEADS, 2 * DIFF_V_DIM, 2 * blk), bf16),
                        pltpu.VMEM((ATTN_GROUP, DIFF_HEADS, 1, 2 * blk), f32),
                        pltpu.VMEM((ATTN_GROUP, DIFF_HEADS, VT_ROWS, 2 * blk), f32)],
        compiler_params=pltpu.CompilerParams(
            dimension_semantics=("parallel", "arbitrary"),
            vmem_limit_bytes=VMEM_LIMIT_BYTES),
        name="diff_attn",
    )(lq1, lk1, lq2, lk2, gain, kbias, qt, k, vt)


def _alibi_key_columns():
    r = jnp.arange(ATTN_BLOCK, dtype=f32)
    slopes = jnp.exp2(-8.0 * jnp.arange(1, DIFF_HEADS + 1, dtype=f32) / DIFF_HEADS)
    cols = jnp.zeros((DIFF_HEADS, ATTN_BLOCK, DIFF_V_DIM), f32)
    cols = cols.at[:, :, 0:2].set((slopes[:, None] * r[None, :])[:, :, None])
    return cols.astype(bf16)


def kernel(x, ffn1_norm, ffn1_w_gate, ffn1_w_up, ffn1_w_down, mix_norm, w_in, sgu_norm, sgu_w, sgu_b, lambda_q1, lambda_k1, lambda_q2, lambda_k2, diff_subln, w_out, ffn2_norm, ffn2_w_gate, ffn2_w_up, ffn2_w_down, final_norm):
    b, s, d = x.shape
    depth = w_in.shape[0]
    t = b * s
    x2d = x.reshape(t, d)
    kbias = _alibi_key_columns()

    as_rows = lambda p: p.reshape(depth, 1, -1)
    wg1, wu1, wd1 = ffn1_w_gate, ffn1_w_up, ffn1_w_down
    wg2, wu2, wd2 = ffn2_w_gate, ffn2_w_up, ffn2_w_down
    win = w_in
    wo = w_out
    sgu_norm_cols = sgu_norm.reshape(depth, SGU_WIDTH, 1)
    sgu_b_rows = sgu_b.reshape(depth, SGU_GROUPS, 1, CHUNK)
    subln_cols = diff_subln.reshape(depth, DIFF_V_DIM, 1)

    for l in range(depth):
        x2d = _ffn_call(l, x2d, as_rows(ffn1_norm), wg1, wu1, wd1)
        yat, qt, k, vt = _inproj_call(l, x2d, s, as_rows(mix_norm), win,
                                      sgu_norm_cols, sgu_w, sgu_b_rows)
        lam_init = 0.8 - 0.6 * math.exp(-0.3 * l)
        ybt = _attn_call(l, as_rows(lambda_q1), as_rows(lambda_k1), as_rows(lambda_q2),
                         as_rows(lambda_k2), subln_cols, kbias,
                         qt, k.reshape(b, s, DIFF_WIDTH), vt, lam_init)
        x2d = _mix_ffn_call(l, x2d, yat, ybt, wo, as_rows(ffn2_norm),
                            wg2, wu2, wd2, final_norm.reshape(1, d),
                            final_norm=(l == depth - 1))
    return x2d.reshape(b, s, d)
```

```python
import functools
import math

import jax
import jax.numpy as jnp
from jax import lax
from jax.experimental import pallas as pl
from jax.experimental.pallas import tpu as pltpu

D_MODEL = 1024
D_FF = 2816
SGU_WIDTH = 512
SGU_GROUP_DIM = 64
SGU_GROUPS = SGU_WIDTH // SGU_GROUP_DIM
CHUNK = 128
DIFF_WIDTH = 512
DIFF_V_DIM = 128
DIFF_QK_DIM = 64
DIFF_HEADS = DIFF_WIDTH // DIFF_V_DIM
IN_COLS = 2 * SGU_WIDTH + 3 * DIFF_WIDTH
EPS = 1e-6
NEG_INF = -1e30
LOG2_E = math.log2(math.e)

VMEM_LIMIT_BYTES = 56 * 1024 * 1024

FFN_TOKENS = 512
FFN_NORM_ROWS = 256
FFN_CHUNKS = ((0, 512), (512, 1024), (1024, 1536), (1536, 2048), (2048, 2560), (2560, 2816))
FFN_CHUNK_MAX = max(c1 - c0 for c0, c1 in FFN_CHUNKS)
assert FFN_CHUNK_MAX == SGU_WIDTH == DIFF_WIDTH
INPROJ_TOKENS = 1024
ATTN_BLOCK = 256
ATTN_GROUP = 4
VT_ROWS = DIFF_V_DIM + 16
ATTN_LOOKAHEAD = 8
ATTN_PV_DELAY = 2

bf16 = jnp.bfloat16
f32 = jnp.float32


def _rmsnorm_rows(x, gain):
    ms = jnp.mean(x * x, axis=-1, keepdims=True)
    return x * lax.rsqrt(ms + EPS) * gain


def _dot(a, b):
    return jnp.dot(a, b, preferred_element_type=f32)


def _dot_nt(a, b):
    return lax.dot_general(a, b, (((1,), (1,)), ((), ())), preferred_element_type=f32)


def _dot_tn(a, b):
    return lax.dot_general(a, b, (((0,), (0,)), ((), ())), preferred_element_type=f32)


def _dot_nt_t(w, h):
    return lax.dot_general(w, h, (((0,), (1,)), ((), ())), preferred_element_type=f32)


class _FfnWeights:
    def __init__(self, layer, hbm, resident, stage_cols, stage_rows, sems):
        self.layer = layer
        self.wg_hbm, self.wu_hbm, self.wd_hbm = hbm
        self.wg, self.wu, self.wd = resident
        self.stage_cols, self.stage_rows, self.sems = stage_cols, stage_rows, sems

    def copies(self, c, kinds=(0, 1, 2)):
        c0, c1 = FFN_CHUNKS[c]
        w, slot = c1 - c0, c % 2
        endpoints = (
            (self.wg_hbm.at[self.layer, :, pl.ds(c0, w)], self.stage_cols.at[slot, 0, :, pl.ds(0, w)]),
            (self.wu_hbm.at[self.layer, :, pl.ds(c0, w)], self.stage_cols.at[slot, 1, :, pl.ds(0, w)]),
            (self.wd_hbm.at[self.layer, pl.ds(c0, w), :], self.stage_rows.at[slot, pl.ds(0, w), :]))
        return [pltpu.make_async_copy(*endpoints[k], self.sems.at[slot, k]) for k in kinds]

    def land(self, c):
        c0, c1 = FFN_CHUNKS[c]
        w, slot = c1 - c0, c % 2
        for copy in self.copies(c):
            copy.wait()
        self.wg[:, c0:c1] = self.stage_cols[slot, 0, :, :w].astype(bf16)
        self.wu[:, c0:c1] = self.stage_cols[slot, 1, :, :w].astype(bf16)
        self.wd[c0:c1, :] = self.stage_rows[slot, :w, :].astype(bf16)
        if c + 2 < len(FFN_CHUNKS):
            for copy in self.copies(c + 2):
                copy.start()


def _swiglu_half_step(x, gain, weights, landing):
    if landing:
        weights.land(0)
    c0, c1 = FFN_CHUNKS[0]
    h_blocks, g_blocks, u_blocks = [], [], []
    for r0 in range(0, x.shape[0], FFN_NORM_ROWS):
        hb = _rmsnorm_rows(x[r0:r0 + FFN_NORM_ROWS], gain).astype(bf16)
        h_blocks.append(hb)
        g_blocks.append(_dot(hb, weights.wg[:, c0:c1]))
        u_blocks.append(_dot(hb, weights.wu[:, c0:c1]))
    h = jnp.concatenate(h_blocks, axis=0)
    acc = None
    for c, (c0, c1) in enumerate(FFN_CHUNKS):
        if c == 0:
            g, u = jnp.concatenate(g_blocks, axis=0), jnp.concatenate(u_blocks, axis=0)
        else:
            if landing:
                weights.land(c)
            g = _dot(h, weights.wg[:, c0:c1])
            u = _dot(h, weights.wu[:, c0:c1])
        a = (g / (1.0 + jnp.exp(-g)) * u).astype(bf16)
        d = _dot(a, weights.wd[c0:c1, :])
        acc = d if acc is None else acc + d
    return x + 0.5 * acc


def _ffn_kernel(x_ref, gain_ref, wg_hbm, wu_hbm, wd_hbm, o_ref,
                wg_ref, wu_ref, wd_ref, stage_cols, stage_rows, sems, *, layer):
    first_step = pl.program_id(0) == 0
    weights = _FfnWeights(layer, (wg_hbm, wu_hbm, wd_hbm), (wg_ref, wu_ref, wd_ref),
                          stage_cols, stage_rows, sems)

    @pl.when(first_step)
    def _():
        for copy in weights.copies(0) + weights.copies(1):
            copy.start()
        o_ref[...] = _swiglu_half_step(x_ref[...], gain_ref[...], weights, landing=True)

    @pl.when(jnp.logical_not(first_step))
    def _():
        o_ref[...] = _swiglu_half_step(x_ref[...], gain_ref[...], weights, landing=False)


def _mix_ffn_kernel(x_ref, yat_ref, ybt_ref, wo_hbm, gain_ref, wg_hbm, wu_hbm, wd_hbm, fgain_ref,
                    o_ref, wo_ref, wg_ref, wu_ref, wd_ref, stage_cols, stage_rows, sems,
                    *, layer, final_norm):
    first_step = pl.program_id(0) == 0
    weights = _FfnWeights(layer, (wg_hbm, wu_hbm, wd_hbm), (wg_ref, wu_ref, wd_ref),
                          stage_cols, stage_rows, sems)

    def stage_out_projection():
        halves = [pltpu.make_async_copy(wo_hbm.at[layer, pl.ds(s * SGU_WIDTH, SGU_WIDTH), :],
                                        stage_rows.at[s], sems.at[s, 2]) for s in range(2)]
        for copy in halves + weights.copies(0, (0, 1)) + weights.copies(1, (0, 1)):
            copy.start()
        for s, copy in enumerate(halves):
            copy.wait()
            wo_ref[s * SGU_WIDTH:(s + 1) * SGU_WIDTH, :] = stage_rows[s].astype(bf16)
        for copy in weights.copies(0, (2,)) + weights.copies(1, (2,)):
            copy.start()

    def step(landing):
        if landing:
            stage_out_projection()
        x = x_ref[...]
        x = (x + _dot_tn(yat_ref[0], wo_ref[:SGU_WIDTH, :])
             + _dot_tn(ybt_ref[0], wo_ref[SGU_WIDTH:, :]))
        x = _swiglu_half_step(x, gain_ref[...], weights, landing)
        if final_norm:
            x = _rmsnorm_rows(x, fgain_ref[...])
        o_ref[...] = x

    pl.when(first_step)(functools.partial(step, True))
    pl.when(jnp.logical_not(first_step))(functools.partial(step, False))


def _const_spec(shape):
    nd = len(shape)
    return pl.BlockSpec(shape, lambda *_: (0,) * nd, pipeline_mode=pl.Buffered(1))


def _layer_spec(layer, shape):
    nd = len(shape)
    return pl.BlockSpec((None,) + tuple(shape), lambda *_: (layer,) + (0,) * nd,
                        pipeline_mode=pl.Buffered(1))


_HBM_SPEC = pl.BlockSpec(memory_space=pl.ANY)


def _ffn_weight_scratch():
    return [pltpu.VMEM((D_MODEL, D_FF), bf16), pltpu.VMEM((D_MODEL, D_FF), bf16),
            pltpu.VMEM((D_FF, D_MODEL), bf16),
            pltpu.VMEM((2, 2, D_MODEL, FFN_CHUNK_MAX), f32),
            pltpu.VMEM((2, FFN_CHUNK_MAX, D_MODEL), f32),
            pltpu.SemaphoreType.DMA((2, 3))]


def _ffn_call(layer, x2d, gain, wg, wu, wd):
    t = x2d.shape[0]
    tm = FFN_TOKENS
    row_spec = pl.BlockSpec((tm, D_MODEL), lambda i: (i, 0))
    return pl.pallas_call(
        functools.partial(_ffn_kernel, layer=layer),
        grid=(t // tm,),
        in_specs=[row_spec, _layer_spec(layer, (1, D_MODEL)), _HBM_SPEC, _HBM_SPEC, _HBM_SPEC],
        out_specs=row_spec,
        out_shape=jax.ShapeDtypeStruct((t, D_MODEL), f32),
        scratch_shapes=_ffn_weight_scratch(),
        compiler_params=pltpu.CompilerParams(
            dimension_semantics=("arbitrary",), vmem_limit_bytes=VMEM_LIMIT_BYTES),
        name="ffn",
    )(x2d, gain, wg, wu, wd)


def _mix_ffn_call(layer, x2d, yat, ybt, wo, gain, wg, wu, wd, fgain, final_norm):
    t = x2d.shape[0]
    tm = FFN_TOKENS
    tiles_per_seq = ybt.shape[2] // tm
    row_spec = pl.BlockSpec((tm, D_MODEL), lambda i: (i, 0))
    cm_spec = pl.BlockSpec((1, SGU_WIDTH, tm),
                           lambda i: (i // tiles_per_seq, 0, i % tiles_per_seq))
    return pl.pallas_call(
        functools.partial(_mix_ffn_kernel, layer=layer, final_norm=final_norm),
        grid=(t // tm,),
        in_specs=[row_spec, cm_spec, cm_spec, _HBM_SPEC, _layer_spec(layer, (1, D_MODEL)),
                  _HBM_SPEC, _HBM_SPEC, _HBM_SPEC, _const_spec((1, D_MODEL))],
        out_specs=row_spec,
        out_shape=jax.ShapeDtypeStruct((t, D_MODEL), f32),
        scratch_shapes=[pltpu.VMEM((2 * SGU_WIDTH, D_MODEL), bf16)] + _ffn_weight_scratch(),
        compiler_params=pltpu.CompilerParams(
            dimension_semantics=("arbitrary",), vmem_limit_bytes=VMEM_LIMIT_BYTES),
        name="mix_ffn",
    )(x2d, yat, ybt, wo, gain, wg, wu, wd, fgain)


def _gelu_exact(x):
    return 0.5 * x * (1.0 + lax.erf(x * (1.0 / math.sqrt(2.0))))


_COL_ZV, _COL_ZU, _COL_Q, _COL_K, _COL_V = (
    SGU_WIDTH, 0, 2 * SGU_WIDTH, 2 * SGU_WIDTH + DIFF_WIDTH, 2 * SGU_WIDTH + 2 * DIFF_WIDTH)
INPROJ_COL_ORDER = (_COL_ZV, _COL_ZU, _COL_Q, _COL_K, _COL_V)
INPROJ_COL_WIDTH = 512
assert SGU_WIDTH == DIFF_WIDTH == INPROJ_COL_WIDTH


def _inproj_kernel(x_ref, gain_ref, win_hbm, sgain_ref, ws_ref, bs_ref,
                   yat_ref, qt_ref, k_ref, vt_ref, win_ref, stage_ref, sems, *, layer):
    tm = x_ref.shape[0]
    width = INPROJ_COL_WIDTH
    first_step = pl.program_id(0) == 0

    def copy(n):
        return pltpu.make_async_copy(
            win_hbm.at[layer, :, pl.ds(INPROJ_COL_ORDER[n], width)],
            stage_ref.at[n % 2], sems.at[n % 2])

    def step(landing):
        def columns(n):
            c0 = INPROJ_COL_ORDER[n]
            if landing:
                copy(n).wait()
                win_ref[:, c0:c0 + width] = stage_ref[n % 2].astype(bf16)
                if n + 2 < len(INPROJ_COL_ORDER):
                    copy(n + 2).start()
            return win_ref[:, c0:c0 + width]

        if landing:
            copy(0).start()
            copy(1).start()
        h = _rmsnorm_rows(x_ref[...], gain_ref[...]).astype(bf16)

        n_chunks = tm // CHUNK
        v = _gelu_exact(_dot_nt_t(columns(0), h))
        zu = _dot_nt_t(columns(1), h)
        stacked = []
        for g in range(SGU_GROUPS):
            lo, hi = g * SGU_GROUP_DIM, (g + 1) * SGU_GROUP_DIM
            vg = v[lo:hi]
            ms = jnp.mean(vg * vg, axis=0, keepdims=True)
            vn = (vg * lax.rsqrt(ms + EPS) * sgain_ref[lo:hi]).astype(bf16)
            stacked.append(jnp.concatenate(
                [vn[:, c * CHUNK:(c + 1) * CHUNK] for c in range(n_chunks)], axis=0))
        qt = _dot_nt_t(columns(2), h)
        u = _gelu_exact(zu)

        row = lax.broadcasted_iota(jnp.int32, (CHUNK, CHUNK), 0)
        col = lax.broadcasted_iota(jnp.int32, (CHUNK, CHUNK), 1)
        causal = col <= row
        gates = []
        for g in range(SGU_GROUPS):
            w = jnp.where(causal, ws_ref[g], 0.0).astype(bf16)
            gates.append(_dot_nt(stacked[g], w) + bs_ref[g])
        k = _dot(h, columns(3))
        qt_ref[0] = (qt * (DIFF_QK_DIM ** -0.5 * LOG2_E)).astype(bf16)
        for g in range(SGU_GROUPS):
            lo, hi = g * SGU_GROUP_DIM, (g + 1) * SGU_GROUP_DIM
            for c in range(n_chunks):
                gate = gates[g][c * SGU_GROUP_DIM:(c + 1) * SGU_GROUP_DIM]
                yat_ref[0, lo:hi, c * CHUNK:(c + 1) * CHUNK] = (
                    u[lo:hi, c * CHUNK:(c + 1) * CHUNK] * gate).astype(bf16)

        vt = _dot_nt_t(columns(4), h)
        k_ref[...] = k.astype(bf16)
        vt = vt.astype(bf16)
        ones_rows = jnp.ones((VT_ROWS - DIFF_V_DIM, ATTN_BLOCK), bf16)
        for hd in range(DIFF_HEADS):
            for c in range(tm // ATTN_BLOCK):
                vt_ref[0, hd, c, :DIFF_V_DIM, :] = vt[hd * DIFF_V_DIM:(hd + 1) * DIFF_V_DIM,
                                                      c * ATTN_BLOCK:(c + 1) * ATTN_BLOCK]
                vt_ref[0, hd, c, DIFF_V_DIM:, :] = ones_rows

    pl.when(first_step)(functools.partial(step, True))
    pl.when(jnp.logical_not(first_step))(functools.partial(step, False))


def _inproj_call(layer, x2d, seq, gain, win, sgain, ws, bs):
    t = x2d.shape[0]
    tm = INPROJ_TOKENS
    tiles_per_seq = seq // tm
    row_spec = pl.BlockSpec((tm, D_MODEL), lambda i: (i, 0))
    half_spec = pl.BlockSpec((tm, SGU_WIDTH), lambda i: (i, 0))
    half_shape = jax.ShapeDtypeStruct((t, SGU_WIDTH), bf16)
    cm_spec = pl.BlockSpec((1, SGU_WIDTH, tm), lambda i: (i // tiles_per_seq, 0, i % tiles_per_seq))
    cm_shape = jax.ShapeDtypeStruct((t // seq, SGU_WIDTH, seq), bf16)
    vt_spec = pl.BlockSpec((1, DIFF_HEADS, tm // ATTN_BLOCK, VT_ROWS, ATTN_BLOCK),
                           lambda i: (i // tiles_per_seq, 0, i % tiles_per_seq, 0, 0))
    vt_shape = jax.ShapeDtypeStruct(
        (t // seq, DIFF_HEADS, seq // ATTN_BLOCK, VT_ROWS, ATTN_BLOCK), bf16)
    return pl.pallas_call(
        functools.partial(_inproj_kernel, layer=layer),
        grid=(t // tm,),
        in_specs=[row_spec, _layer_spec(layer, (1, D_MODEL)), _HBM_SPEC,
                  _layer_spec(layer, (SGU_WIDTH, 1)),
                  _layer_spec(layer, (SGU_GROUPS, CHUNK, CHUNK)),
                  _layer_spec(layer, (SGU_GROUPS, 1, CHUNK))],
        out_specs=[cm_spec, cm_spec, half_spec, vt_spec],
        out_shape=[cm_shape, cm_shape, half_shape, vt_shape],
        scratch_shapes=[pltpu.VMEM((D_MODEL, IN_COLS), bf16),
                        pltpu.VMEM((2, D_MODEL, INPROJ_COL_WIDTH), f32),
                        pltpu.SemaphoreType.DMA((2,))],
        compiler_params=pltpu.CompilerParams(
            dimension_semantics=("arbitrary",), vmem_limit_bytes=VMEM_LIMIT_BYTES),
        name="inproj_sgu",
    )(x2d, gain, win, sgain, ws, bs)


def _attn_kernel(lq1_ref, lk1_ref, lq2_ref, lk2_ref, gain_ref, kbias_ref, qt_ref, k_ref, vt_ref,
                 o_ref, qop_ref, m_ref, acc_ref, *, lam_init):
    blk = ATTN_BLOCK
    group = pl.program_id(1)
    first_block = group * ATTN_GROUP
    heads = range(DIFF_HEADS)
    slopes = [2.0 ** (-8.0 * (hd + 1) / DIFF_HEADS) * LOG2_E for hd in heads]
    krow = lax.broadcasted_iota(jnp.int32, (blk, 2 * blk), 0)
    qcol = lax.broadcasted_iota(jnp.int32, (blk, 2 * blk), 1) & (blk - 1)
    causal = krow <= qcol

    row = lax.broadcasted_iota(jnp.int32, (DIFF_V_DIM, 2 * blk), 0)
    c1 = jnp.asarray(LOG2_E, f32).astype(bf16).astype(f32)
    q_tail = jnp.where(row == 0, c1, jnp.where(row == 1, LOG2_E - c1, 0.0)).astype(bf16)
    q_zero = jnp.zeros((DIFF_QK_DIM, blk), bf16)
    for qs in range(ATTN_GROUP):
        for hd in heads:
            qt = qt_ref[0, hd * DIFF_V_DIM:(hd + 1) * DIFF_V_DIM, qs * blk:(qs + 1) * blk]
            qop_ref[qs, hd, :DIFF_QK_DIM, :blk] = qt[:DIFF_QK_DIM]
            qop_ref[qs, hd, :DIFF_QK_DIM, blk:] = q_zero
            qop_ref[qs, hd, DIFF_QK_DIM:DIFF_V_DIM, :blk] = q_zero
            qop_ref[qs, hd, DIFF_QK_DIM:DIFF_V_DIM, blk:] = qt[DIFF_QK_DIM:]
            qop_ref[qs, hd, DIFF_V_DIM:, :] = q_tail

    def scores(item):
        qs, kb, hd, diagonal = item
        k_rows = k_ref[0, pl.ds(pl.multiple_of(kb * blk, blk), blk),
                       hd * DIFF_V_DIM:(hd + 1) * DIFF_V_DIM]
        s = _dot(jnp.concatenate([k_rows, kbias_ref[hd]], axis=1), qop_ref[qs, hd])
        if diagonal:
            s = jnp.where(causal, s, NEG_INF)
        return s

    def softmax(item, s):
        qs, kb, hd, diagonal = item
        s_max = jnp.max(s, axis=0, keepdims=True)
        if diagonal:
            m_new = s_max
            p, alpha = jnp.exp2(s - m_new), None
        else:
            beta = slopes[hd] * ((kb - (first_block + qs)) * blk).astype(f32)
            m_old = m_ref[qs, hd]
            m_new = jnp.maximum(m_old, s_max + beta)
            p, alpha = jnp.exp2(s - (m_new - beta)), jnp.exp2(m_old - m_new)
        m_ref[qs, hd] = m_new
        return item, p.astype(bf16), alpha

    def accumulate(item, p, alpha):
        qs, kb, hd, _ = item
        pv = _dot(vt_ref[0, hd, kb], p)
        acc_ref[qs, hd] = pv if alpha is None else alpha * acc_ref[qs, hd] + pv

    def run(items):
        scored = [scores(item) for item in items[:ATTN_LOOKAHEAD]]
        weighted = []
        for n, item in enumerate(items):
            if n + ATTN_LOOKAHEAD < len(items):
                scored.append(scores(items[n + ATTN_LOOKAHEAD]))
            weighted.append(softmax(item, scored.pop(0)))
            if len(weighted) > ATTN_PV_DELAY:
                accumulate(*weighted.pop(0))
        for args in weighted:
            accumulate(*args)

    diagonal_region = []
    for rnd in range(ATTN_GROUP):
        for qs in range(rnd, ATTN_GROUP):
            key = qs if rnd == 0 else rnd - 1
            diagonal_region += [(qs, first_block + key, hd, rnd == 0) for hd in heads]
    run(diagonal_region)

    def earlier_keys(j, carry):
        run([(qs, 2 * j + kbo, hd, False)
             for kbo in range(2) for qs in range(ATTN_GROUP) for hd in heads])
        return carry

    lax.fori_loop(0, first_block // 2, earlier_keys, 0)

    lam = (jnp.exp(jnp.sum(lq1_ref[...] * lk1_ref[...], axis=-1, keepdims=True))
           - jnp.exp(jnp.sum(lq2_ref[...] * lk2_ref[...], axis=-1, keepdims=True))
           + lam_init)
    gain = gain_ref[...] * (1.0 - lam_init)
    for qs in range(ATTN_GROUP):
        for hd in heads:
            acc1, acc2 = acc_ref[qs, hd, :, :blk], acc_ref[qs, hd, :, blk:]
            o = (acc1[:DIFF_V_DIM] / acc1[DIFF_V_DIM:DIFF_V_DIM + 1]
                 - lam * (acc2[:DIFF_V_DIM] / acc2[DIFF_V_DIM:DIFF_V_DIM + 1]))
            ms = jnp.mean(o * o, axis=0, keepdims=True)
            o_ref[0, hd * DIFF_V_DIM:(hd + 1) * DIFF_V_DIM, qs * blk:(qs + 1) * blk] = (
                o * lax.rsqrt(ms + EPS) * gain).astype(bf16)


def _attn_call(layer, lq1, lk1, lq2, lk2, gain, kbias, qt, k, vt, lam_init):
    b, s, _ = k.shape
    blk = ATTN_BLOCK
    lam_spec = _layer_spec(layer, (1, DIFF_QK_DIM))
    tq = ATTN_GROUP * blk
    q_spec = pl.BlockSpec((1, DIFF_WIDTH, tq), lambda bi, gi: (bi, 0, gi))
    k_spec = pl.BlockSpec((1, s, DIFF_WIDTH), lambda bi, gi: (bi, 0, 0))
    vt_spec = pl.BlockSpec((1, DIFF_HEADS, s // blk, VT_ROWS, blk),
                           lambda bi, gi: (bi, 0, 0, 0, 0))
    return pl.pallas_call(
        functools.partial(_attn_kernel, lam_init=lam_init),
        grid=(b, s // tq),
        in_specs=[lam_spec, lam_spec, lam_spec, lam_spec, _layer_spec(layer, (DIFF_V_DIM, 1)),
                  _const_spec((DIFF_HEADS, blk, DIFF_V_DIM)), q_spec, k_spec, vt_spec],
        out_specs=pl.BlockSpec((1, DIFF_WIDTH, tq), lambda bi, gi: (bi, 0, gi)),
        out_shape=jax.ShapeDtypeStruct((b, DIFF_WIDTH, s), bf16),
        scratch_shapes=[pltpu.VMEM((ATTN_GROUP, DIFF_HEADS, 2 * DIFF_V_DIM, 2 * blk), bf16),
                        pltpu.VMEM((ATTN_GROUP, DIFF_HEADS, 1, 2 * blk), f32),
                        pltpu.VMEM((ATTN_GROUP, DIFF_HEADS, VT_ROWS, 2 * blk), f32)],
        compiler_params=pltpu.CompilerParams(
            dimension_semantics=("parallel", "arbitrary"),
            vmem_limit_bytes=VMEM_LIMIT_BYTES),
        name="diff_attn",
    )(lq1, lk1, lq2, lk2, gain, kbias, qt, k, vt)


def _alibi_key_columns():
    r = jnp.arange(ATTN_BLOCK, dtype=f32)
    slopes = jnp.exp2(-8.0 * jnp.arange(1, DIFF_HEADS + 1, dtype=f32) / DIFF_HEADS)
    cols = jnp.zeros((DIFF_HEADS, ATTN_BLOCK, DIFF_V_DIM), f32)
    cols = cols.at[:, :, 0:2].set((slopes[:, None] * r[None, :])[:, :, None])
    return cols.astype(bf16)


def kernel(x, ffn1_norm, ffn1_w_gate, ffn1_w_up, ffn1_w_down, mix_norm, w_in, sgu_norm, sgu_w, sgu_b, lambda_q1, lambda_k1, lambda_q2, lambda_k2, diff_subln, w_out, ffn2_norm, ffn2_w_gate, ffn2_w_up, ffn2_w_down, final_norm):
    b, s, d = x.shape
    depth = w_in.shape[0]
    t = b * s
    x2d = x.reshape(t, d)
    kbias = _alibi_key_columns()

    as_rows = lambda p: p.reshape(depth, 1, -1)
    wg1, wu1, wd1 = ffn1_w_gate, ffn1_w_up, ffn1_w_down
    wg2, wu2, wd2 = ffn2_w_gate, ffn2_w_up, ffn2_w_down
    win = w_in
    wo = w_out
    sgu_norm_cols = sgu_norm.reshape(depth, SGU_WIDTH, 1)
    sgu_b_rows = sgu_b.reshape(depth, SGU_GROUPS, 1, CHUNK)
    subln_cols = diff_subln.reshape(depth, DIFF_V_DIM, 1)

    for l in range(depth):
        x2d = _ffn_call(l, x2d, as_rows(ffn1_norm), wg1, wu1, wd1)
        yat, qt, k, vt = _inproj_call(l, x2d, s, as_rows(mix_norm), win,
                                      sgu_norm_cols, sgu_w, sgu_b_rows)
        lam_init = 0.8 - 0.6 * math.exp(-0.3 * l)
        ybt = _attn_call(l, as_rows(lambda_q1), as_rows(lambda_k1), as_rows(lambda_q2),
                         as_rows(lambda_k2), subln_cols, kbias,
                         qt, k.reshape(b, s, DIFF_WIDTH), vt, lam_init)
        x2d = _mix_ffn_call(l, x2d, yat, ybt, wo, as_rows(ffn2_norm),
                            wg2, wu2, wd2, final_norm.reshape(1, d),
                            final_norm=(l == depth - 1))
    return x2d.reshape(b, s, d)
```

```python
import functools
import math

import jax
import jax.numpy as jnp
from jax import lax
from jax.experimental import pallas as pl
from jax.experimental.pallas import tpu as pltpu

D_MODEL = 1024
D_FF = 2816
SGU_WIDTH = 512
SGU_GROUP_DIM = 64
SGU_GROUPS = SGU_WIDTH // SGU_GROUP_DIM
CHUNK = 128
DIFF_WIDTH = 512
DIFF_V_DIM = 128
DIFF_QK_DIM = 64
DIFF_HEADS = DIFF_WIDTH // DIFF_V_DIM
IN_COLS = 2 * SGU_WIDTH + 3 * DIFF_WIDTH
EPS = 1e-6
NEG_INF = -1e30
LOG2_E = math.log2(math.e)

VMEM_LIMIT_BYTES = 56 * 1024 * 1024

FFN_TOKENS = 512
FFN_NORM_ROWS = 256
FFN_CHUNKS = ((0, 512), (512, 1024), (1024, 1536), (1536, 2048), (2048, 2560), (2560, 2816))
FFN_CHUNK_MAX = max(c1 - c0 for c0, c1 in FFN_CHUNKS)
assert FFN_CHUNK_MAX == SGU_WIDTH == DIFF_WIDTH
INPROJ_TOKENS = 1024
ATTN_BLOCK = 256
ATTN_GROUP = 4
VT_ROWS = DIFF_V_DIM + 16
ATTN_LOOKAHEAD = 8
ATTN_PV_DELAY = 2

bf16 = jnp.bfloat16
f32 = jnp.float32


def _rmsnorm_rows(x, gain):
    ms = jnp.mean(x * x, axis=-1, keepdims=True)
    return x * lax.rsqrt(ms + EPS) * gain


def _dot(a, b):
    return jnp.dot(a, b, preferred_element_type=f32)


def _dot_nt(a, b):
    return lax.dot_general(a, b, (((1,), (1,)), ((), ())), preferred_element_type=f32)


def _dot_tn(a, b):
    return lax.dot_general(a, b, (((0,), (0,)), ((), ())), preferred_element_type=f32)


def _dot_nt_t(w, h):
    return lax.dot_general(w, h, (((0,), (1,)), ((), ())), preferred_element_type=f32)


class _FfnWeights:
    def __init__(self, layer, hbm, resident, stage_cols, stage_rows, sems):
        self.layer = layer
        self.wg_hbm, self.wu_hbm, self.wd_hbm = hbm
        self.wg, self.wu, self.wd = resident
        self.stage_cols, self.stage_rows, self.sems = stage_cols, stage_rows, sems

    def copies(self, c, kinds=(0, 1, 2)):
        c0, c1 = FFN_CHUNKS[c]
        w, slot = c1 - c0, c % 2
        endpoints = (
            (self.wg_hbm.at[self.layer, :, pl.ds(c0, w)], self.stage_cols.at[slot, 0, :, pl.ds(0, w)]),
            (self.wu_hbm.at[self.layer, :, pl.ds(c0, w)], self.stage_cols.at[slot, 1, :, pl.ds(0, w)]),
            (self.wd_hbm.at[self.layer, pl.ds(c0, w), :], self.stage_rows.at[slot, pl.ds(0, w), :]))
        return [pltpu.make_async_copy(*endpoints[k], self.sems.at[slot, k]) for k in kinds]

    def land(self, c):
        c0, c1 = FFN_CHUNKS[c]
        w, slot = c1 - c0, c % 2
        for copy in self.copies(c):
            copy.wait()
        self.wg[:, c0:c1] = self.stage_cols[slot, 0, :, :w].astype(bf16)
        self.wu[:, c0:c1] = self.stage_cols[slot, 1, :, :w].astype(bf16)
        self.wd[c0:c1, :] = self.stage_rows[slot, :w, :].astype(bf16)
        if c + 2 < len(FFN_CHUNKS):
            for copy in self.copies(c + 2):
                copy.start()


def _swiglu_half_step(x, gain, weights, landing):
    if landing:
        weights.land(0)
    c0, c1 = FFN_CHUNKS[0]
    h_blocks, g_blocks, u_blocks = [], [], []
    for r0 in range(0, x.shape[0], FFN_NORM_ROWS):
        hb = _rmsnorm_rows(x[r0:r0 + FFN_NORM_ROWS], gain).astype(bf16)
        h_blocks.append(hb)
        g_blocks.append(_dot(hb, weights.wg[:, c0:c1]))
        u_blocks.append(_dot(hb, weights.wu[:, c0:c1]))
    h = jnp.concatenate(h_blocks, axis=0)
    acc = None
    for c, (c0, c1) in enumerate(FFN_CHUNKS):
        if c == 0:
            g, u = jnp.concatenate(g_blocks, axis=0), jnp.concatenate(u_blocks, axis=0)
        else:
            if landing:
                weights.land(c)
            g = _dot(h, weights.wg[:, c0:c1])
            u = _dot(h, weights.wu[:, c0:c1])
        a = (g / (1.0 + jnp.exp(-g)) * u).astype(bf16)
        d = _dot(a, weights.wd[c0:c1, :])
        acc = d if acc is None else acc + d
    return x + 0.5 * acc


def _ffn_kernel(x_ref, gain_ref, wg_hbm, wu_hbm, wd_hbm, o_ref,
                wg_ref, wu_ref, wd_ref, stage_cols, stage_rows, sems, *, layer):
    first_step = pl.program_id(0) == 0
    weights = _FfnWeights(layer, (wg_hbm, wu_hbm, wd_hbm), (wg_ref, wu_ref, wd_ref),
                          stage_cols, stage_rows, sems)

    @pl.when(first_step)
    def _():
        for copy in weights.copies(0) + weights.copies(1):
            copy.start()
        o_ref[...] = _swiglu_half_step(x_ref[...], gain_ref[...], weights, landing=True)

    @pl.when(jnp.logical_not(first_step))
    def _():
        o_ref[...] = _swiglu_half_step(x_ref[...], gain_ref[...], weights, landing=False)


def _mix_ffn_kernel(x_ref, yat_ref, ybt_ref, wo_hbm, gain_ref, wg_hbm, wu_hbm, wd_hbm, fgain_ref,
                    o_ref, wo_ref, wg_ref, wu_ref, wd_ref, stage_cols, stage_rows, sems,
                    *, layer, final_norm):
    first_step = pl.program_id(0) == 0
    weights = _FfnWeights(layer, (wg_hbm, wu_hbm, wd_hbm), (wg_ref, wu_ref, wd_ref),
                          stage_cols, stage_rows, sems)

    def stage_out_projection():
        halves = [pltpu.make_async_copy(wo_hbm.at[layer, pl.ds(s * SGU_WIDTH, SGU_WIDTH), :],
                                        stage_rows.at[s], sems.at[s, 2]) for s in range(2)]
        for copy in halves + weights.copies(0, (0, 1)) + weights.copies(1, (0, 1)):
            copy.start()
        for s, copy in enumerate(halves):
            copy.wait()
            wo_ref[s * SGU_WIDTH:(s + 1) * SGU_WIDTH, :] = stage_rows[s].astype(bf16)
        for copy in weights.copies(0, (2,)) + weights.copies(1, (2,)):
            copy.start()

    def step(landing):
        if landing:
            stage_out_projection()
        x = jnp.concatenate(
            [x_ref[r0:r0 + FFN_NORM_ROWS]
             + _dot_tn(yat_ref[0, :, r0:r0 + FFN_NORM_ROWS], wo_ref[:SGU_WIDTH, :])
             + _dot_tn(ybt_ref[0, :, r0:r0 + FFN_NORM_ROWS], wo_ref[SGU_WIDTH:, :])
             for r0 in range(0, x_ref.shape[0], FFN_NORM_ROWS)], axis=0)
        x = _swiglu_half_step(x, gain_ref[...], weights, landing)
        if final_norm:
            x = _rmsnorm_rows(x, fgain_ref[...])
        o_ref[...] = x

    pl.when(first_step)(functools.partial(step, True))
    pl.when(jnp.logical_not(first_step))(functools.partial(step, False))


def _const_spec(shape):
    nd = len(shape)
    return pl.BlockSpec(shape, lambda *_: (0,) * nd, pipeline_mode=pl.Buffered(1))


def _layer_spec(layer, shape):
    nd = len(shape)
    return pl.BlockSpec((None,) + tuple(shape), lambda *_: (layer,) + (0,) * nd,
                        pipeline_mode=pl.Buffered(1))


_HBM_SPEC = pl.BlockSpec(memory_space=pl.ANY)


def _ffn_weight_scratch():
    return [pltpu.VMEM((D_MODEL, D_FF), bf16), pltpu.VMEM((D_MODEL, D_FF), bf16),
            pltpu.VMEM((D_FF, D_MODEL), bf16),
            pltpu.VMEM((2, 2, D_MODEL, FFN_CHUNK_MAX), f32),
            pltpu.VMEM((2, FFN_CHUNK_MAX, D_MODEL), f32),
            pltpu.SemaphoreType.DMA((2, 3))]


def _ffn_call(layer, x2d, gain, wg, wu, wd):
    t = x2d.shape[0]
    tm = FFN_TOKENS
    row_spec = pl.BlockSpec((tm, D_MODEL), lambda i: (i, 0))
    return pl.pallas_call(
        functools.partial(_ffn_kernel, layer=layer),
        grid=(t // tm,),
        in_specs=[row_spec, _layer_spec(layer, (1, D_MODEL)), _HBM_SPEC, _HBM_SPEC, _HBM_SPEC],
        out_specs=row_spec,
        out_shape=jax.ShapeDtypeStruct((t, D_MODEL), f32),
        scratch_shapes=_ffn_weight_scratch(),
        compiler_params=pltpu.CompilerParams(
            dimension_semantics=("arbitrary",), vmem_limit_bytes=VMEM_LIMIT_BYTES),
        name="ffn",
    )(x2d, gain, wg, wu, wd)


def _mix_ffn_call(layer, x2d, yat, ybt, wo, gain, wg, wu, wd, fgain, final_norm):
    t = x2d.shape[0]
    tm = FFN_TOKENS
    tiles_per_seq = ybt.shape[2] // tm
    row_spec = pl.BlockSpec((tm, D_MODEL), lambda i: (i, 0))
    cm_spec = pl.BlockSpec((1, SGU_WIDTH, tm),
                           lambda i: (i // tiles_per_seq, 0, i % tiles_per_seq))
    return pl.pallas_call(
        functools.partial(_mix_ffn_kernel, layer=layer, final_norm=final_norm),
        grid=(t // tm,),
        in_specs=[row_spec, cm_spec, cm_spec, _HBM_SPEC, _layer_spec(layer, (1, D_MODEL)),
                  _HBM_SPEC, _HBM_SPEC, _HBM_SPEC, _const_spec((1, D_MODEL))],
        out_specs=row_spec,
        out_shape=jax.ShapeDtypeStruct((t, D_MODEL), f32),
        scratch_shapes=[pltpu.VMEM((2 * SGU_WIDTH, D_MODEL), bf16)] + _ffn_weight_scratch(),
        compiler_params=pltpu.CompilerParams(
            dimension_semantics=("arbitrary",), vmem_limit_bytes=VMEM_LIMIT_BYTES),
        name="mix_ffn",
    )(x2d, yat, ybt, wo, gain, wg, wu, wd, fgain)


def _gelu_exact(x):
    return 0.5 * x * (1.0 + lax.erf(x * (1.0 / math.sqrt(2.0))))


_COL_ZV, _COL_ZU, _COL_Q, _COL_K, _COL_V = (
    SGU_WIDTH, 0, 2 * SGU_WIDTH, 2 * SGU_WIDTH + DIFF_WIDTH, 2 * SGU_WIDTH + 2 * DIFF_WIDTH)
INPROJ_COL_ORDER = (_COL_ZV, _COL_ZU, _COL_Q, _COL_K, _COL_V)
INPROJ_COL_WIDTH = 512
assert SGU_WIDTH == DIFF_WIDTH == INPROJ_COL_WIDTH


def _inproj_kernel(x_ref, gain_ref, win_hbm, sgain_ref, ws_ref, bs_ref,
                   yat_ref, qt_ref, k_ref, vt_ref, win_ref, stage_ref, sems, *, layer):
    tm = x_ref.shape[0]
    width = INPROJ_COL_WIDTH
    first_step = pl.program_id(0) == 0

    def copy(n):
        return pltpu.make_async_copy(
            win_hbm.at[layer, :, pl.ds(INPROJ_COL_ORDER[n], width)],
            stage_ref.at[n % 2], sems.at[n % 2])

    def step(landing):
        def columns(n):
            c0 = INPROJ_COL_ORDER[n]
            if landing:
                copy(n).wait()
                win_ref[:, c0:c0 + width] = stage_ref[n % 2].astype(bf16)
                if n + 2 < len(INPROJ_COL_ORDER):
                    copy(n + 2).start()
            return win_ref[:, c0:c0 + width]

        if landing:
            copy(0).start()
            copy(1).start()
        h = _rmsnorm_rows(x_ref[...], gain_ref[...]).astype(bf16)

        n_chunks = tm // CHUNK
        v = _gelu_exact(_dot_nt_t(columns(0), h))
        zu = _dot_nt_t(columns(1), h)
        stacked = []
        for g in range(SGU_GROUPS):
            lo, hi = g * SGU_GROUP_DIM, (g + 1) * SGU_GROUP_DIM
            vg = v[lo:hi]
            ms = jnp.mean(vg * vg, axis=0, keepdims=True)
            vn = (vg * lax.rsqrt(ms + EPS) * sgain_ref[lo:hi]).astype(bf16)
            stacked.append(jnp.concatenate(
                [vn[:, c * CHUNK:(c + 1) * CHUNK] for c in range(n_chunks)], axis=0))
        qt = _dot_nt_t(columns(2), h)
        u = _gelu_exact(zu)

        row = lax.broadcasted_iota(jnp.int32, (CHUNK, CHUNK), 0)
        col = lax.broadcasted_iota(jnp.int32, (CHUNK, CHUNK), 1)
        causal = col <= row
        gates = []
        for g in range(SGU_GROUPS):
            w = jnp.where(causal, ws_ref[g], 0.0).astype(bf16)
            gates.append(_dot_nt(stacked[g], w) + bs_ref[g])
        k = _dot(h, columns(3))
        qt_ref[0] = (qt * (DIFF_QK_DIM ** -0.5 * LOG2_E)).astype(bf16)
        for g in range(SGU_GROUPS):
            lo, hi = g * SGU_GROUP_DIM, (g + 1) * SGU_GROUP_DIM
            for c in range(n_chunks):
                gate = gates[g][c * SGU_GROUP_DIM:(c + 1) * SGU_GROUP_DIM]
                yat_ref[0, lo:hi, c * CHUNK:(c + 1) * CHUNK] = (
                    u[lo:hi, c * CHUNK:(c + 1) * CHUNK] * gate).astype(bf16)

        vt = _dot_nt_t(columns(4), h)
        k_ref[...] = k.astype(bf16)
        vt = vt.astype(bf16)
        ones_rows = jnp.ones((VT_ROWS - DIFF_V_DIM, ATTN_BLOCK), bf16)
        for hd in range(DIFF_HEADS):
            for c in range(tm // ATTN_BLOCK):
                vt_ref[0, hd, c, :DIFF_V_DIM, :] = vt[hd * DIFF_V_DIM:(hd + 1) * DIFF_V_DIM,
                                                      c * ATTN_BLOCK:(c + 1) * ATTN_BLOCK]
                vt_ref[0, hd, c, DIFF_V_DIM:, :] = ones_rows

    pl.when(first_step)(functools.partial(step, True))
    pl.when(jnp.logical_not(first_step))(functools.partial(step, False))


def _inproj_call(layer, x2d, seq, gain, win, sgain, ws, bs):
    t = x2d.shape[0]
    tm = INPROJ_TOKENS
    tiles_per_seq = seq // tm
    row_spec = pl.BlockSpec((tm, D_MODEL), lambda i: (i, 0))
    half_spec = pl.BlockSpec((tm, SGU_WIDTH), lambda i: (i, 0))
    half_shape = jax.ShapeDtypeStruct((t, SGU_WIDTH), bf16)
    cm_spec = pl.BlockSpec((1, SGU_WIDTH, tm), lambda i: (i // tiles_per_seq, 0, i % tiles_per_seq))
    cm_shape = jax.ShapeDtypeStruct((t // seq, SGU_WIDTH, seq), bf16)
    vt_spec = pl.BlockSpec((1, DIFF_HEADS, tm // ATTN_BLOCK, VT_ROWS, ATTN_BLOCK),
                           lambda i: (i // tiles_per_seq, 0, i % tiles_per_seq, 0, 0))
    vt_shape = jax.ShapeDtypeStruct(
        (t // seq, DIFF_HEADS, seq // ATTN_BLOCK, VT_ROWS, ATTN_BLOCK), bf16)
    return pl.pallas_call(
        functools.partial(_inproj_kernel, layer=layer),
        grid=(t // tm,),
        in_specs=[row_spec, _layer_spec(layer, (1, D_MODEL)), _HBM_SPEC,
                  _layer_spec(layer, (SGU_WIDTH, 1)),
                  _layer_spec(layer, (SGU_GROUPS, CHUNK, CHUNK)),
                  _layer_spec(layer, (SGU_GROUPS, 1, CHUNK))],
        out_specs=[cm_spec, cm_spec, half_spec, vt_spec],
        out_shape=[cm_shape, cm_shape, half_shape, vt_shape],
        scratch_shapes=[pltpu.VMEM((D_MODEL, IN_COLS), bf16),
                        pltpu.VMEM((2, D_MODEL, INPROJ_COL_WIDTH), f32),
                        pltpu.SemaphoreType.DMA((2,))],
        compiler_params=pltpu.CompilerParams(
            dimension_semantics=("arbitrary",), vmem_limit_bytes=VMEM_LIMIT_BYTES),
        name="inproj_sgu",
    )(x2d, gain, win, sgain, ws, bs)


def _attn_kernel(lq1_ref, lk1_ref, lq2_ref, lk2_ref, gain_ref, kbias_ref, qt_ref, k_ref, vt_ref,
                 o_ref, qop_ref, m_ref, acc_ref, *, lam_init):
    blk = ATTN_BLOCK
    group = pl.program_id(1)
    first_block = group * ATTN_GROUP
    heads = range(DIFF_HEADS)
    slopes = [2.0 ** (-8.0 * (hd + 1) / DIFF_HEADS) * LOG2_E for hd in heads]
    krow = lax.broadcasted_iota(jnp.int32, (blk, 2 * blk), 0)
    qcol = lax.broadcasted_iota(jnp.int32, (blk, 2 * blk), 1) & (blk - 1)
    causal = krow <= qcol

    row = lax.broadcasted_iota(jnp.int32, (DIFF_V_DIM, 2 * blk), 0)
    c1 = jnp.asarray(LOG2_E, f32).astype(bf16).astype(f32)
    q_tail = jnp.where(row == 0, c1, jnp.where(row == 1, LOG2_E - c1, 0.0)).astype(bf16)
    q_zero = jnp.zeros((DIFF_QK_DIM, blk), bf16)
    for qs in range(ATTN_GROUP):
        for hd in heads:
            qt = qt_ref[0, hd * DIFF_V_DIM:(hd + 1) * DIFF_V_DIM, qs * blk:(qs + 1) * blk]
            qop_ref[qs, hd, :DIFF_QK_DIM, :blk] = qt[:DIFF_QK_DIM]
            qop_ref[qs, hd, :DIFF_QK_DIM, blk:] = q_zero
            qop_ref[qs, hd, DIFF_QK_DIM:DIFF_V_DIM, :blk] = q_zero
            qop_ref[qs, hd, DIFF_QK_DIM:DIFF_V_DIM, blk:] = qt[DIFF_QK_DIM:]
            qop_ref[qs, hd, DIFF_V_DIM:, :] = q_tail

    def scores(item):
        qs, kb, hd, diagonal = item
        k_rows = k_ref[0, pl.ds(pl.multiple_of(kb * blk, blk), blk),
                       hd * DIFF_V_DIM:(hd + 1) * DIFF_V_DIM]
        s = _dot(jnp.concatenate([k_rows, kbias_ref[hd]], axis=1), qop_ref[qs, hd])
        if diagonal:
            s = jnp.where(causal, s, NEG_INF)
        return s

    def softmax(item, s):
        qs, kb, hd, diagonal = item
        s_max = jnp.max(s, axis=0, keepdims=True)
        if diagonal:
            m_new = s_max
            p, alpha = jnp.exp2(s - m_new), None
        else:
            beta = slopes[hd] * ((kb - (first_block + qs)) * blk).astype(f32)
            m_old = m_ref[qs, hd]
            m_new = jnp.maximum(m_old, s_max + beta)
            p, alpha = jnp.exp2(s - (m_new - beta)), jnp.exp2(m_old - m_new)
        m_ref[qs, hd] = m_new
        return item, p.astype(bf16), alpha

    def accumulate(item, p, alpha):
        qs, kb, hd, _ = item
        pv = _dot(vt_ref[0, hd, kb], p)
        acc_ref[qs, hd] = pv if alpha is None else alpha * acc_ref[qs, hd] + pv

    def run(items):
        scored = [scores(item) for item in items[:ATTN_LOOKAHEAD]]
        weighted = []
        for n, item in enumerate(items):
            if n + ATTN_LOOKAHEAD < len(items):
                scored.append(scores(items[n + ATTN_LOOKAHEAD]))
            weighted.append(softmax(item, scored.pop(0)))
            if len(weighted) > ATTN_PV_DELAY:
                accumulate(*weighted.pop(0))
        for args in weighted:
            accumulate(*args)

    diagonal_region = []
    for rnd in range(ATTN_GROUP):
        for qs in range(rnd, ATTN_GROUP):
            key = qs if rnd == 0 else rnd - 1
            diagonal_region += [(qs, first_block + key, hd, rnd == 0) for hd in heads]
    run(diagonal_region)

    def earlier_keys(j, carry):
        run([(qs, 2 * j + kbo, hd, False)
             for kbo in range(2) for qs in range(ATTN_GROUP) for hd in heads])
        return carry

    lax.fori_loop(0, first_block // 2, earlier_keys, 0)

    lam = (jnp.exp(jnp.sum(lq1_ref[...] * lk1_ref[...], axis=-1, keepdims=True))
           - jnp.exp(jnp.sum(lq2_ref[...] * lk2_ref[...], axis=-1, keepdims=True))
           + lam_init)
    gain = gain_ref[...] * (1.0 - lam_init)
    for qs in range(ATTN_GROUP):
        for hd in heads:
            acc1, acc2 = acc_ref[qs, hd, :, :blk], acc_ref[qs, hd, :, blk:]
            o = (acc1[:DIFF_V_DIM] / acc1[DIFF_V_DIM:DIFF_V_DIM + 1]
                 - lam * (acc2[:DIFF_V_DIM] / acc2[DIFF_V_DIM:DIFF_V_DIM + 1]))
            ms = jnp.mean(o * o, axis=0, keepdims=True)
            o_ref[0, hd * DIFF_V_DIM:(hd + 1) * DIFF_V_DIM, qs * blk:(qs + 1) * blk] = (
                o * lax.rsqrt(ms + EPS) * gain).astype(bf16)


def _attn_call(layer, lq1, lk1, lq2, lk2, gain, kbias, qt, k, vt, lam_init):
    b, s, _ = k.shape
    blk = ATTN_BLOCK
    lam_spec = _layer_spec(layer, (1, DIFF_QK_DIM))
    tq = ATTN_GROUP * blk
    q_spec = pl.BlockSpec((1, DIFF_WIDTH, tq), lambda bi, gi: (bi, 0, gi))
    k_spec = pl.BlockSpec((1, s, DIFF_WIDTH), lambda bi, gi: (bi, 0, 0))
    vt_spec = pl.BlockSpec((1, DIFF_HEADS, s // blk, VT_ROWS, blk),
                           lambda bi, gi: (bi, 0, 0, 0, 0))
    return pl.pallas_call(
        functools.partial(_attn_kernel, lam_init=lam_init),
        grid=(b, s // tq),
        in_specs=[lam_spec, lam_spec, lam_spec, lam_spec, _layer_spec(layer, (DIFF_V_DIM, 1)),
                  _const_spec((DIFF_HEADS, blk, DIFF_V_DIM)), q_spec, k_spec, vt_spec],
        out_specs=pl.BlockSpec((1, DIFF_WIDTH, tq), lambda bi, gi: (bi, 0, gi)),
        out_shape=jax.ShapeDtypeStruct((b, DIFF_WIDTH, s), bf16),
        scratch_shapes=[pltpu.VMEM((ATTN_GROUP, DIFF_HEADS, 2 * DIFF_V_DIM, 2 * blk), bf16),
                        pltpu.VMEM((ATTN_GROUP, DIFF_HEADS, 1, 2 * blk), f32),
                        pltpu.VMEM((ATTN_GROUP, DIFF_HEADS, VT_ROWS, 2 * blk), f32)],
        compiler_params=pltpu.CompilerParams(
            dimension_semantics=("parallel", "arbitrary"),
            vmem_limit_bytes=VMEM_LIMIT_BYTES),
        name="diff_attn",
    )(lq1, lk1, lq2, lk2, gain, kbias, qt, k, vt)


def _alibi_key_columns():
    r = jnp.arange(ATTN_BLOCK, dtype=f32)
    slopes = jnp.exp2(-8.0 * jnp.arange(1, DIFF_HEADS + 1, dtype=f32) / DIFF_HEADS)
    cols = jnp.zeros((DIFF_HEADS, ATTN_BLOCK, DIFF_V_DIM), f32)
    cols = cols.at[:, :, 0:2].set((slopes[:, None] * r[None, :])[:, :, None])
    return cols.astype(bf16)


def kernel(x, ffn1_norm, ffn1_w_gate, ffn1_w_up, ffn1_w_down, mix_norm, w_in, sgu_norm, sgu_w, sgu_b, lambda_q1, lambda_k1, lambda_q2, lambda_k2, diff_subln, w_out, ffn2_norm, ffn2_w_gate, ffn2_w_up, ffn2_w_down, final_norm):
    b, s, d = x.shape
    depth = w_in.shape[0]
    t = b * s
    x2d = x.reshape(t, d)
    kbias = _alibi_key_columns()

    as_rows = lambda p: p.reshape(depth, 1, -1)
    wg1, wu1, wd1 = ffn1_w_gate, ffn1_w_up, ffn1_w_down
    wg2, wu2, wd2 = ffn2_w_gate, ffn2_w_up, ffn2_w_down
    win = w_in
    wo = w_out
    sgu_norm_cols = sgu_norm.reshape(depth, SGU_WIDTH, 1)
    sgu_b_rows = sgu_b.reshape(depth, SGU_GROUPS, 1, CHUNK)
    subln_cols = diff_subln.reshape(depth, DIFF_V_DIM, 1)

    for l in range(depth):
        x2d = _ffn_call(l, x2d, as_rows(ffn1_norm), wg1, wu1, wd1)
        yat, qt, k, vt = _inproj_call(l, x2d, s, as_rows(mix_norm), win,
                                      sgu_norm_cols, sgu_w, sgu_b_rows)
        lam_init = 0.8 - 0.6 * math.exp(-0.3 * l)
        ybt = _attn_call(l, as_rows(lambda_q1), as_rows(lambda_k1), as_rows(lambda_q2),
                         as_rows(lambda_k2), subln_cols, kbias,
                         qt, k.reshape(b, s, DIFF_WIDTH), vt, lam_init)
        x2d = _mix_ffn_call(l, x2d, yat, ybt, wo, as_rows(ffn2_norm),
                            wg2, wu2, wd2, final_norm.reshape(1, d),
                            final_norm=(l == depth - 1))
    return x2d.reshape(b, s, d)
```

```python
import functools
import math

import jax
import jax.numpy as jnp
from jax import lax
from jax.experimental import pallas as pl
from jax.experimental.pallas import tpu as pltpu

D_MODEL = 1024
D_FF = 2816
SGU_WIDTH = 512
SGU_GROUP_DIM = 64
SGU_GROUPS = SGU_WIDTH // SGU_GROUP_DIM
CHUNK = 128
DIFF_WIDTH = 512
DIFF_V_DIM = 128
DIFF_QK_DIM = 64
DIFF_HEADS = DIFF_WIDTH // DIFF_V_DIM
IN_COLS = 2 * SGU_WIDTH + 3 * DIFF_WIDTH
EPS = 1e-6
NEG_INF = -1e30
LOG2_E = math.log2(math.e)

VMEM_LIMIT_BYTES = 56 * 1024 * 1024

FFN_TOKENS = 512
FFN_NORM_ROWS = 256
FFN_CHUNKS = ((0, 512), (512, 1024), (1024, 1536), (1536, 2048), (2048, 2560), (2560, 2816))
FFN_CHUNK_MAX = max(c1 - c0 for c0, c1 in FFN_CHUNKS)
assert FFN_CHUNK_MAX == SGU_WIDTH == DIFF_WIDTH
INPROJ_TOKENS = 1024
ATTN_BLOCK = 256
ATTN_GROUP = 4
VT_ROWS = DIFF_V_DIM + 16
ATTN_LOOKAHEAD = 8
ATTN_PV_DELAY = 2

bf16 = jnp.bfloat16
f32 = jnp.float32


def _rmsnorm_rows(x, gain):
    ms = jnp.mean(x * x, axis=-1, keepdims=True)
    return x * lax.rsqrt(ms + EPS) * gain


def _dot(a, b):
    return jnp.dot(a, b, preferred_element_type=f32)


def _dot_nt(a, b):
    return lax.dot_general(a, b, (((1,), (1,)), ((), ())), preferred_element_type=f32)


def _dot_tn(a, b):
    return lax.dot_general(a, b, (((0,), (0,)), ((), ())), preferred_element_type=f32)


def _dot_nt_t(w, h):
    return lax.dot_general(w, h, (((0,), (1,)), ((), ())), preferred_element_type=f32)


class _FfnWeights:
    def __init__(self, layer, hbm, resident, stage_cols, stage_rows, sems):
        self.layer = layer
        self.wg_hbm, self.wu_hbm, self.wd_hbm = hbm
        self.wg, self.wu, self.wd = resident
        self.stage_cols, self.stage_rows, self.sems = stage_cols, stage_rows, sems

    def copies(self, c, kinds=(0, 1, 2)):
        c0, c1 = FFN_CHUNKS[c]
        w, slot = c1 - c0, c % 2
        endpoints = (
            (self.wg_hbm.at[self.layer, :, pl.ds(c0, w)], self.stage_cols.at[slot, 0, :, pl.ds(0, w)]),
            (self.wu_hbm.at[self.layer, :, pl.ds(c0, w)], self.stage_cols.at[slot, 1, :, pl.ds(0, w)]),
            (self.wd_hbm.at[self.layer, pl.ds(c0, w), :], self.stage_rows.at[slot, pl.ds(0, w), :]))
        return [pltpu.make_async_copy(*endpoints[k], self.sems.at[slot, k]) for k in kinds]

    def land(self, c):
        c0, c1 = FFN_CHUNKS[c]
        w, slot = c1 - c0, c % 2
        for copy in self.copies(c):
            copy.wait()
        self.wg[c, :, :w] = self.stage_cols[slot, 0, :, :w].astype(bf16)
        self.wu[c, :, :w] = self.stage_cols[slot, 1, :, :w].astype(bf16)
        self.wd[c0:c1, :] = self.stage_rows[slot, :w, :].astype(bf16)
        if c + 2 < len(FFN_CHUNKS):
            for copy in self.copies(c + 2):
                copy.start()


def _swiglu_half_step(x, gain, weights, landing):
    if landing:
        weights.land(0)
    c0, c1 = FFN_CHUNKS[0]
    h_blocks, g_blocks, u_blocks = [], [], []
    for r0 in range(0, x.shape[0], FFN_NORM_ROWS):
        hb = _rmsnorm_rows(x[r0:r0 + FFN_NORM_ROWS], gain).astype(bf16)
        h_blocks.append(hb)
        g_blocks.append(_dot(hb, weights.wg[0, :, :c1 - c0]))
        u_blocks.append(_dot(hb, weights.wu[0, :, :c1 - c0]))
    h = jnp.concatenate(h_blocks, axis=0)
    acc = None
    for c, (c0, c1) in enumerate(FFN_CHUNKS):
        if c == 0:
            g, u = jnp.concatenate(g_blocks, axis=0), jnp.concatenate(u_blocks, axis=0)
        else:
            if landing:
                weights.land(c)
            g = _dot(h, weights.wg[c, :, :c1 - c0])
            u = _dot(h, weights.wu[c, :, :c1 - c0])
        a = (g / (1.0 + jnp.exp(-g)) * u).astype(bf16)
        d = _dot(a, weights.wd[c0:c1, :])
        acc = d if acc is None else acc + d
    return x + 0.5 * acc


def _ffn_kernel(x_ref, gain_ref, wg_hbm, wu_hbm, wd_hbm, o_ref,
                wg_ref, wu_ref, wd_ref, stage_cols, stage_rows, sems, *, layer):
    first_step = pl.program_id(0) == 0
    weights = _FfnWeights(layer, (wg_hbm, wu_hbm, wd_hbm), (wg_ref, wu_ref, wd_ref),
                          stage_cols, stage_rows, sems)

    @pl.when(first_step)
    def _():
        for copy in weights.copies(0) + weights.copies(1):
            copy.start()
        o_ref[...] = _swiglu_half_step(x_ref[...], gain_ref[...], weights, landing=True)

    @pl.when(jnp.logical_not(first_step))
    def _():
        o_ref[...] = _swiglu_half_step(x_ref[...], gain_ref[...], weights, landing=False)


def _mix_ffn_kernel(x_ref, yat_ref, ybt_ref, wo_hbm, gain_ref, wg_hbm, wu_hbm, wd_hbm, fgain_ref,
                    o_ref, wo_ref, wg_ref, wu_ref, wd_ref, stage_cols, stage_rows, sems,
                    *, layer, final_norm):
    first_step = pl.program_id(0) == 0
    weights = _FfnWeights(layer, (wg_hbm, wu_hbm, wd_hbm), (wg_ref, wu_ref, wd_ref),
                          stage_cols, stage_rows, sems)

    def stage_out_projection():
        halves = [pltpu.make_async_copy(wo_hbm.at[layer, pl.ds(s * SGU_WIDTH, SGU_WIDTH), :],
                                        stage_rows.at[s], sems.at[s, 2]) for s in range(2)]
        for copy in halves + weights.copies(0, (0, 1)) + weights.copies(1, (0, 1)):
            copy.start()
        for s, copy in enumerate(halves):
            copy.wait()
            wo_ref[s * SGU_WIDTH:(s + 1) * SGU_WIDTH, :] = stage_rows[s].astype(bf16)
        for copy in weights.copies(0, (2,)) + weights.copies(1, (2,)):
            copy.start()

    def step(landing):
        if landing:
            stage_out_projection()
        x = jnp.concatenate(
            [x_ref[r0:r0 + FFN_NORM_ROWS]
             + _dot_tn(yat_ref[0, :, r0:r0 + FFN_NORM_ROWS], wo_ref[:SGU_WIDTH, :])
             + _dot_tn(ybt_ref[0, :, r0:r0 + FFN_NORM_ROWS], wo_ref[SGU_WIDTH:, :])
             for r0 in range(0, x_ref.shape[0], FFN_NORM_ROWS)], axis=0)
        x = _swiglu_half_step(x, gain_ref[...], weights, landing)
        if final_norm:
            x = _rmsnorm_rows(x, fgain_ref[...])
        o_ref[...] = x

    pl.when(first_step)(functools.partial(step, True))
    pl.when(jnp.logical_not(first_step))(functools.partial(step, False))


def _const_spec(shape):
    nd = len(shape)
    return pl.BlockSpec(shape, lambda *_: (0,) * nd, pipeline_mode=pl.Buffered(1))


def _layer_spec(layer, shape):
    nd = len(shape)
    return pl.BlockSpec((None,) + tuple(shape), lambda *_: (layer,) + (0,) * nd,
                        pipeline_mode=pl.Buffered(1))


_HBM_SPEC = pl.BlockSpec(memory_space=pl.ANY)


def _ffn_weight_scratch():
    chunk_major = pltpu.VMEM((len(FFN_CHUNKS), D_MODEL, FFN_CHUNK_MAX), bf16)
    return [chunk_major, chunk_major,
            pltpu.VMEM((D_FF, D_MODEL), bf16),
            pltpu.VMEM((2, 2, D_MODEL, FFN_CHUNK_MAX), f32),
            pltpu.VMEM((2, FFN_CHUNK_MAX, D_MODEL), f32),
            pltpu.SemaphoreType.DMA((2, 3))]


def _ffn_call(layer, x2d, gain, wg, wu, wd):
    t = x2d.shape[0]
    tm = FFN_TOKENS
    row_spec = pl.BlockSpec((tm, D_MODEL), lambda i: (i, 0))
    return pl.pallas_call(
        functools.partial(_ffn_kernel, layer=layer),
        grid=(t // tm,),
        in_specs=[row_spec, _layer_spec(layer, (1, D_MODEL)), _HBM_SPEC, _HBM_SPEC, _HBM_SPEC],
        out_specs=row_spec,
        out_shape=jax.ShapeDtypeStruct((t, D_MODEL), f32),
        scratch_shapes=_ffn_weight_scratch(),
        compiler_params=pltpu.CompilerParams(
            dimension_semantics=("arbitrary",), vmem_limit_bytes=VMEM_LIMIT_BYTES),
        name="ffn",
    )(x2d, gain, wg, wu, wd)


def _mix_ffn_call(layer, x2d, yat, ybt, wo, gain, wg, wu, wd, fgain, final_norm):
    t = x2d.shape[0]
    tm = FFN_TOKENS
    tiles_per_seq = ybt.shape[2] // tm
    row_spec = pl.BlockSpec((tm, D_MODEL), lambda i: (i, 0))
    cm_spec = pl.BlockSpec((1, SGU_WIDTH, tm),
                           lambda i: (i // tiles_per_seq, 0, i % tiles_per_seq))
    return pl.pallas_call(
        functools.partial(_mix_ffn_kernel, layer=layer, final_norm=final_norm),
        grid=(t // tm,),
        in_specs=[row_spec, cm_spec, cm_spec, _HBM_SPEC, _layer_spec(layer, (1, D_MODEL)),
                  _HBM_SPEC, _HBM_SPEC, _HBM_SPEC, _const_spec((1, D_MODEL))],
        out_specs=row_spec,
        out_shape=jax.ShapeDtypeStruct((t, D_MODEL), f32),
        scratch_shapes=[pltpu.VMEM((2 * SGU_WIDTH, D_MODEL), bf16)] + _ffn_weight_scratch(),
        compiler_params=pltpu.CompilerParams(
            dimension_semantics=("arbitrary",), vmem_limit_bytes=VMEM_LIMIT_BYTES),
        name="mix_ffn",
    )(x2d, yat, ybt, wo, gain, wg, wu, wd, fgain)


def _gelu_exact(x):
    return 0.5 * x * (1.0 + lax.erf(x * (1.0 / math.sqrt(2.0))))


_COL_ZV, _COL_ZU, _COL_Q, _COL_K, _COL_V = (
    SGU_WIDTH, 0, 2 * SGU_WIDTH, 2 * SGU_WIDTH + DIFF_WIDTH, 2 * SGU_WIDTH + 2 * DIFF_WIDTH)
INPROJ_COL_ORDER = (_COL_ZV, _COL_ZU, _COL_Q, _COL_K, _COL_V)
INPROJ_COL_WIDTH = 512
assert SGU_WIDTH == DIFF_WIDTH == INPROJ_COL_WIDTH


def _inproj_kernel(x_ref, gain_ref, win_hbm, sgain_ref, ws_ref, bs_ref,
                   yat_ref, qt_ref, k_ref, vt_ref, win_ref, stage_ref, sems, *, layer):
    tm = x_ref.shape[0]
    width = INPROJ_COL_WIDTH
    first_step = pl.program_id(0) == 0

    def copy(n):
        return pltpu.make_async_copy(
            win_hbm.at[layer, :, pl.ds(INPROJ_COL_ORDER[n], width)],
            stage_ref.at[n % 2], sems.at[n % 2])

    def step(landing):
        def columns(n):
            if landing:
                copy(n).wait()
                win_ref[n] = stage_ref[n % 2].astype(bf16)
                if n + 2 < len(INPROJ_COL_ORDER):
                    copy(n + 2).start()
            return win_ref[n]

        if landing:
            copy(0).start()
            copy(1).start()
        h = _rmsnorm_rows(x_ref[...], gain_ref[...]).astype(bf16)

        n_chunks = tm // CHUNK
        v = _gelu_exact(_dot_nt_t(columns(0), h))
        zu = _dot_nt_t(columns(1), h)
        stacked = []
        for g in range(SGU_GROUPS):
            lo, hi = g * SGU_GROUP_DIM, (g + 1) * SGU_GROUP_DIM
            vg = v[lo:hi]
            ms = jnp.mean(vg * vg, axis=0, keepdims=True)
            vn = (vg * lax.rsqrt(ms + EPS) * sgain_ref[lo:hi]).astype(bf16)
            stacked.append(jnp.concatenate(
                [vn[:, c * CHUNK:(c + 1) * CHUNK] for c in range(n_chunks)], axis=0))
        qt = _dot_nt_t(columns(2), h)
        u = _gelu_exact(zu)

        row = lax.broadcasted_iota(jnp.int32, (CHUNK, CHUNK), 0)
        col = lax.broadcasted_iota(jnp.int32, (CHUNK, CHUNK), 1)
        causal = col <= row
        gates = []
        for g in range(SGU_GROUPS):
            w = jnp.where(causal, ws_ref[g], 0.0).astype(bf16)
            gates.append(_dot_nt(stacked[g], w) + bs_ref[g])
        k = _dot(h, columns(3))
        qt_ref[0] = (qt * (DIFF_QK_DIM ** -0.5 * LOG2_E)).astype(bf16)
        for g in range(SGU_GROUPS):
            lo, hi = g * SGU_GROUP_DIM, (g + 1) * SGU_GROUP_DIM
            for c in range(n_chunks):
                gate = gates[g][c * SGU_GROUP_DIM:(c + 1) * SGU_GROUP_DIM]
                yat_ref[0, lo:hi, c * CHUNK:(c + 1) * CHUNK] = (
                    u[lo:hi, c * CHUNK:(c + 1) * CHUNK] * gate).astype(bf16)

        vt = _dot_nt_t(columns(4), h)
        for hd in range(DIFF_HEADS):
            k_ref[0, hd] = k[:, hd * DIFF_V_DIM:(hd + 1) * DIFF_V_DIM].astype(bf16)
        vt = vt.astype(bf16)
        ones_rows = jnp.ones((VT_ROWS - DIFF_V_DIM, ATTN_BLOCK), bf16)
        for hd in range(DIFF_HEADS):
            for c in range(tm // ATTN_BLOCK):
                vt_ref[0, hd, c, :DIFF_V_DIM, :] = vt[hd * DIFF_V_DIM:(hd + 1) * DIFF_V_DIM,
                                                      c * ATTN_BLOCK:(c + 1) * ATTN_BLOCK]
                vt_ref[0, hd, c, DIFF_V_DIM:, :] = ones_rows

    pl.when(first_step)(functools.partial(step, True))
    pl.when(jnp.logical_not(first_step))(functools.partial(step, False))


def _inproj_call(layer, x2d, seq, gain, win, sgain, ws, bs):
    t = x2d.shape[0]
    tm = INPROJ_TOKENS
    tiles_per_seq = seq // tm
    row_spec = pl.BlockSpec((tm, D_MODEL), lambda i: (i, 0))
    cm_spec = pl.BlockSpec((1, SGU_WIDTH, tm), lambda i: (i // tiles_per_seq, 0, i % tiles_per_seq))
    cm_shape = jax.ShapeDtypeStruct((t // seq, SGU_WIDTH, seq), bf16)
    vt_spec = pl.BlockSpec((1, DIFF_HEADS, tm // ATTN_BLOCK, VT_ROWS, ATTN_BLOCK),
                           lambda i: (i // tiles_per_seq, 0, i % tiles_per_seq, 0, 0))
    vt_shape = jax.ShapeDtypeStruct(
        (t // seq, DIFF_HEADS, seq // ATTN_BLOCK, VT_ROWS, ATTN_BLOCK), bf16)
    return pl.pallas_call(
        functools.partial(_inproj_kernel, layer=layer),
        grid=(t // tm,),
        in_specs=[row_spec, _layer_spec(layer, (1, D_MODEL)), _HBM_SPEC,
                  _layer_spec(layer, (SGU_WIDTH, 1)),
                  _layer_spec(layer, (SGU_GROUPS, CHUNK, CHUNK)),
                  _layer_spec(layer, (SGU_GROUPS, 1, CHUNK))],
        out_specs=[cm_spec, cm_spec,
                   pl.BlockSpec((1, DIFF_HEADS, tm, DIFF_V_DIM),
                                lambda i: (i // tiles_per_seq, 0, i % tiles_per_seq, 0)),
                   vt_spec],
        out_shape=[cm_shape, cm_shape,
                   jax.ShapeDtypeStruct((t // seq, DIFF_HEADS, seq, DIFF_V_DIM), bf16), vt_shape],
        scratch_shapes=[pltpu.VMEM((len(INPROJ_COL_ORDER), D_MODEL, INPROJ_COL_WIDTH), bf16),
                        pltpu.VMEM((2, D_MODEL, INPROJ_COL_WIDTH), f32),
                        pltpu.SemaphoreType.DMA((2,))],
        compiler_params=pltpu.CompilerParams(
            dimension_semantics=("arbitrary",), vmem_limit_bytes=VMEM_LIMIT_BYTES),
        name="inproj_sgu",
    )(x2d, gain, win, sgain, ws, bs)


def _attn_kernel(lq1_ref, lk1_ref, lq2_ref, lk2_ref, gain_ref, kbias_ref, qt_ref, k_ref, vt_ref,
                 o_ref, qop_ref, m_ref, acc_ref, *, lam_init):
    blk = ATTN_BLOCK
    group = pl.program_id(1)
    first_block = group * ATTN_GROUP
    heads = range(DIFF_HEADS)
    slopes = [2.0 ** (-8.0 * (hd + 1) / DIFF_HEADS) * LOG2_E for hd in heads]
    krow = lax.broadcasted_iota(jnp.int32, (blk, 2 * blk), 0)
    qcol = lax.broadcasted_iota(jnp.int32, (blk, 2 * blk), 1) & (blk - 1)
    causal = krow <= qcol

    row = lax.broadcasted_iota(jnp.int32, (DIFF_V_DIM, 2 * blk), 0)
    c1 = jnp.asarray(LOG2_E, f32).astype(bf16).astype(f32)
    q_tail = jnp.where(row == 0, c1, jnp.where(row == 1, LOG2_E - c1, 0.0)).astype(bf16)
    q_zero = jnp.zeros((DIFF_QK_DIM, blk), bf16)
    for qs in range(ATTN_GROUP):
        for hd in heads:
            qt = qt_ref[0, hd * DIFF_V_DIM:(hd + 1) * DIFF_V_DIM, qs * blk:(qs + 1) * blk]
            qop_ref[qs, hd, :DIFF_QK_DIM, :blk] = qt[:DIFF_QK_DIM]
            qop_ref[qs, hd, :DIFF_QK_DIM, blk:] = q_zero
            qop_ref[qs, hd, DIFF_QK_DIM:DIFF_V_DIM, :blk] = q_zero
            qop_ref[qs, hd, DIFF_QK_DIM:DIFF_V_DIM, blk:] = qt[DIFF_QK_DIM:]
            qop_ref[qs, hd, DIFF_V_DIM:, :] = q_tail

    def scores(item):
        qs, kb, hd, diagonal = item
        k_rows = k_ref[0, hd, pl.ds(pl.multiple_of(kb * blk, blk), blk), :]
        s = _dot(jnp.concatenate([k_rows, kbias_ref[hd]], axis=1), qop_ref[qs, hd])
        if diagonal:
            s = jnp.where(causal, s, NEG_INF)
        return s

    def softmax(item, s):
        qs, kb, hd, diagonal = item
        s_max = jnp.max(s, axis=0, keepdims=True)
        if diagonal:
            m_new = s_max
            p, alpha = jnp.exp2(s - m_new), None
        else:
            beta = slopes[hd] * ((kb - (first_block + qs)) * blk).astype(f32)
            m_old = m_ref[qs, hd]
            m_new = jnp.maximum(m_old, s_max + beta)
            p, alpha = jnp.exp2(s - (m_new - beta)), jnp.exp2(m_old - m_new)
        m_ref[qs, hd] = m_new
        return item, p.astype(bf16), alpha

    def accumulate(item, p, alpha):
        qs, kb, hd, _ = item
        pv = _dot(vt_ref[0, hd, kb], p)
        acc_ref[qs, hd] = pv if alpha is None else alpha * acc_ref[qs, hd] + pv

    def run(items):
        scored = [scores(item) for item in items[:ATTN_LOOKAHEAD]]
        weighted = []
        for n, item in enumerate(items):
            if n + ATTN_LOOKAHEAD < len(items):
                scored.append(scores(items[n + ATTN_LOOKAHEAD]))
            weighted.append(softmax(item, scored.pop(0)))
            if len(weighted) > ATTN_PV_DELAY:
                accumulate(*weighted.pop(0))
        for args in weighted:
            accumulate(*args)

    diagonal_region = []
    for rnd in range(ATTN_GROUP):
        for qs in range(rnd, ATTN_GROUP):
            key = qs if rnd == 0 else rnd - 1
            diagonal_region += [(qs, first_block + key, hd, rnd == 0) for hd in heads]
    run(diagonal_region)

    def earlier_keys(j, carry):
        run([(qs, 2 * j + kbo, hd, False)
             for kbo in range(2) for qs in range(ATTN_GROUP) for hd in heads])
        return carry

    lax.fori_loop(0, first_block // 2, earlier_keys, 0)

    lam = (jnp.exp(jnp.sum(lq1_ref[...] * lk1_ref[...], axis=-1, keepdims=True))
           - jnp.exp(jnp.sum(lq2_ref[...] * lk2_ref[...], axis=-1, keepdims=True))
           + lam_init)
    gain = gain_ref[...] * (1.0 - lam_init)
    for qs in range(ATTN_GROUP):
        for hd in heads:
            acc1, acc2 = acc_ref[qs, hd, :, :blk], acc_ref[qs, hd, :, blk:]
            o = (acc1[:DIFF_V_DIM] / acc1[DIFF_V_DIM:DIFF_V_DIM + 1]
                 - lam * (acc2[:DIFF_V_DIM] / acc2[DIFF_V_DIM:DIFF_V_DIM + 1]))
            ms = jnp.mean(o * o, axis=0, keepdims=True)
            o_ref[0, hd * DIFF_V_DIM:(hd + 1) * DIFF_V_DIM, qs * blk:(qs + 1) * blk] = (
                o * lax.rsqrt(ms + EPS) * gain).astype(bf16)


def _attn_call(layer, lq1, lk1, lq2, lk2, gain, kbias, qt, k, vt, lam_init):
    b, _, s, _ = k.shape
    blk = ATTN_BLOCK
    lam_spec = _layer_spec(layer, (1, DIFF_QK_DIM))
    tq = ATTN_GROUP * blk
    q_spec = pl.BlockSpec((1, DIFF_WIDTH, tq), lambda bi, gi: (bi, 0, gi))
    k_spec = pl.BlockSpec((1, DIFF_HEADS, s, DIFF_V_DIM), lambda bi, gi: (bi, 0, 0, 0))
    vt_spec = pl.BlockSpec((1, DIFF_HEADS, s // blk, VT_ROWS, blk),
                           lambda bi, gi: (bi, 0, 0, 0, 0))
    return pl.pallas_call(
        functools.partial(_attn_kernel, lam_init=lam_init),
        grid=(b, s // tq),
        in_specs=[lam_spec, lam_spec, lam_spec, lam_spec, _layer_spec(layer, (DIFF_V_DIM, 1)),
                  _const_spec((DIFF_HEADS, blk, DIFF_V_DIM)), q_spec, k_spec, vt_spec],
        out_specs=pl.BlockSpec((1, DIFF_WIDTH, tq), lambda bi, gi: (bi, 0, gi)),
        out_shape=jax.ShapeDtypeStruct((b, DIFF_WIDTH, s), bf16),
        scratch_shapes=[pltpu.VMEM((ATTN_GROUP, DIFF_HEADS, 2 * DIFF_V_DIM, 2 * blk), bf16),
                        pltpu.VMEM((ATTN_GROUP, DIFF_HEADS, 1, 2 * blk), f32),
                        pltpu.VMEM((ATTN_GROUP, DIFF_HEADS, VT_ROWS, 2 * blk), f32)],
        compiler_params=pltpu.CompilerParams(
            dimension_semantics=("parallel", "arbitrary"),
            vmem_limit_bytes=VMEM_LIMIT_BYTES),
        name="diff_attn",
    )(lq1, lk1, lq2, lk2, gain, kbias, qt, k, vt)


def _alibi_key_columns():
    r = jnp.arange(ATTN_BLOCK, dtype=f32)
    slopes = jnp.exp2(-8.0 * jnp.arange(1, DIFF_HEADS + 1, dtype=f32) / DIFF_HEADS)
    cols = jnp.zeros((DIFF_HEADS, ATTN_BLOCK, DIFF_V_DIM), f32)
    cols = cols.at[:, :, 0:2].set((slopes[:, None] * r[None, :])[:, :, None])
    return cols.astype(bf16)


def kernel(x, ffn1_norm, ffn1_w_gate, ffn1_w_up, ffn1_w_down, mix_norm, w_in, sgu_norm, sgu_w, sgu_b, lambda_q1, lambda_k1, lambda_q2, lambda_k2, diff_subln, w_out, ffn2_norm, ffn2_w_gate, ffn2_w_up, ffn2_w_down, final_norm):
    b, s, d = x.shape
    depth = w_in.shape[0]
    t = b * s
    x2d = x.reshape(t, d)
    kbias = _alibi_key_columns()

    as_rows = lambda p: p.reshape(depth, 1, -1)
    wg1, wu1, wd1 = ffn1_w_gate, ffn1_w_up, ffn1_w_down
    wg2, wu2, wd2 = ffn2_w_gate, ffn2_w_up, ffn2_w_down
    win = w_in
    wo = w_out
    sgu_norm_cols = sgu_norm.reshape(depth, SGU_WIDTH, 1)
    sgu_b_rows = sgu_b.reshape(depth, SGU_GROUPS, 1, CHUNK)
    subln_cols = diff_subln.reshape(depth, DIFF_V_DIM, 1)

    for l in range(depth):
        x2d = _ffn_call(l, x2d, as_rows(ffn1_norm), wg1, wu1, wd1)
        yat, qt, k, vt = _inproj_call(l, x2d, s, as_rows(mix_norm), win,
                                      sgu_norm_cols, sgu_w, sgu_b_rows)
        lam_init = 0.8 - 0.6 * math.exp(-0.3 * l)
        ybt = _attn_call(l, as_rows(lambda_q1), as_rows(lambda_k1), as_rows(lambda_q2),
                         as_rows(lambda_k2), subln_cols, kbias,
                         qt, k, vt, lam_init)
        x2d = _mix_ffn_call(l, x2d, yat, ybt, wo, as_rows(ffn2_norm),
                            wg2, wu2, wd2, final_norm.reshape(1, d),
                            final_norm=(l == depth - 1))
    return x2d.reshape(b, s, d)
```

```python
import functools
import math

import jax
import jax.numpy as jnp
from jax import lax
from jax.experimental import pallas as pl
from jax.experimental.pallas import tpu as pltpu

D_MODEL = 1024
D_FF = 2816
SGU_WIDTH = 512
SGU_GROUP_DIM = 64
SGU_GROUPS = SGU_WIDTH // SGU_GROUP_DIM
CHUNK = 128
DIFF_WIDTH = 512
DIFF_V_DIM = 128
DIFF_QK_DIM = 64
DIFF_HEADS = DIFF_WIDTH // DIFF_V_DIM
IN_COLS = 2 * SGU_WIDTH + 3 * DIFF_WIDTH
EPS = 1e-6
NEG_INF = -1e30
LOG2_E = math.log2(math.e)

VMEM_LIMIT_BYTES = 56 * 1024 * 1024

FFN_TOKENS = 512
FFN_NORM_ROWS = 256
FFN_CHUNKS = ((0, 512), (512, 1024), (1024, 1536), (1536, 2048), (2048, 2560), (2560, 2816))
FFN_CHUNK_MAX = max(c1 - c0 for c0, c1 in FFN_CHUNKS)
assert FFN_CHUNK_MAX == SGU_WIDTH == DIFF_WIDTH
INPROJ_TOKENS = 1024
ATTN_BLOCK = 256
ATTN_GROUP = 4
VT_ROWS = DIFF_V_DIM + 16
ATTN_LOOKAHEAD = 8
ATTN_PV_DELAY = 2

bf16 = jnp.bfloat16
f32 = jnp.float32


def _rmsnorm_rows(x, gain):
    ms = jnp.mean(x * x, axis=-1, keepdims=True)
    return x * lax.rsqrt(ms + EPS) * gain


def _dot(a, b):
    return jnp.dot(a, b, preferred_element_type=f32)


def _dot_nt(a, b):
    return lax.dot_general(a, b, (((1,), (1,)), ((), ())), preferred_element_type=f32)


def _dot_tn(a, b):
    return lax.dot_general(a, b, (((0,), (0,)), ((), ())), preferred_element_type=f32)


def _dot_nt_t(w, h):
    return lax.dot_general(w, h, (((0,), (1,)), ((), ())), preferred_element_type=f32)


class _FfnWeights:
    def __init__(self, layer, hbm, resident, stage_cols, stage_rows, sems):
        self.layer = layer
        self.wg_hbm, self.wu_hbm, self.wd_hbm = hbm
        self.wg, self.wu, self.wd = resident
        self.stage_cols, self.stage_rows, self.sems = stage_cols, stage_rows, sems

    def copies(self, c, kinds=(0, 1, 2)):
        c0, c1 = FFN_CHUNKS[c]
        w, slot = c1 - c0, c % 2
        endpoints = (
            (self.wg_hbm.at[self.layer, :, pl.ds(c0, w)], self.stage_cols.at[slot, 0, :, pl.ds(0, w)]),
            (self.wu_hbm.at[self.layer, :, pl.ds(c0, w)], self.stage_cols.at[slot, 1, :, pl.ds(0, w)]),
            (self.wd_hbm.at[self.layer, pl.ds(c0, w), :], self.stage_rows.at[slot, pl.ds(0, w), :]))
        return [pltpu.make_async_copy(*endpoints[k], self.sems.at[slot, k]) for k in kinds]

    def land(self, c):
        c0, c1 = FFN_CHUNKS[c]
        w, slot = c1 - c0, c % 2
        for copy in self.copies(c):
            copy.wait()
        self.wg[c, :, :w] = self.stage_cols[slot, 0, :, :w].astype(bf16)
        self.wu[c, :, :w] = self.stage_cols[slot, 1, :, :w].astype(bf16)
        self.wd[c0:c1, :] = self.stage_rows[slot, :w, :].astype(bf16)
        if c + 2 < len(FFN_CHUNKS):
            for copy in self.copies(c + 2):
                copy.start()


def _swiglu_half_step(x, gain, weights, landing):
    if landing:
        weights.land(0)
    c0, c1 = FFN_CHUNKS[0]
    h_blocks, g_blocks, u_blocks = [], [], []
    for r0 in range(0, x.shape[0], FFN_NORM_ROWS):
        hb = _rmsnorm_rows(x[r0:r0 + FFN_NORM_ROWS], gain).astype(bf16)
        h_blocks.append(hb)
        g_blocks.append(_dot(hb, weights.wg[0, :, :c1 - c0]))
        u_blocks.append(_dot(hb, weights.wu[0, :, :c1 - c0]))
    h = jnp.concatenate(h_blocks, axis=0)
    acc = None
    for c, (c0, c1) in enumerate(FFN_CHUNKS):
        if c == 0:
            g, u = jnp.concatenate(g_blocks, axis=0), jnp.concatenate(u_blocks, axis=0)
        else:
            if landing:
                weights.land(c)
            g = _dot(h, weights.wg[c, :, :c1 - c0])
            u = _dot(h, weights.wu[c, :, :c1 - c0])
        a = (g / (1.0 + jnp.exp(-g)) * u).astype(bf16)
        d = _dot(a, weights.wd[c0:c1, :])
        acc = d if acc is None else acc + d
    return x + 0.5 * acc


def _ffn_kernel(x_ref, gain_ref, wg_hbm, wu_hbm, wd_hbm, o_ref,
                wg_ref, wu_ref, wd_ref, stage_cols, stage_rows, sems, *, layer):
    first_step = pl.program_id(0) == 0
    weights = _FfnWeights(layer, (wg_hbm, wu_hbm, wd_hbm), (wg_ref, wu_ref, wd_ref),
                          stage_cols, stage_rows, sems)

    @pl.when(first_step)
    def _():
        for copy in weights.copies(0) + weights.copies(1):
            copy.start()
        o_ref[...] = _swiglu_half_step(x_ref[...], gain_ref[...], weights, landing=True)

    @pl.when(jnp.logical_not(first_step))
    def _():
        o_ref[...] = _swiglu_half_step(x_ref[...], gain_ref[...], weights, landing=False)


def _mix_ffn_kernel(x_ref, yat_ref, ybt_ref, wo_hbm, gain_ref, wg_hbm, wu_hbm, wd_hbm, fgain_ref,
                    o_ref, wo_ref, wg_ref, wu_ref, wd_ref, stage_cols, stage_rows, sems,
                    *, layer, final_norm):
    first_step = pl.program_id(0) == 0
    weights = _FfnWeights(layer, (wg_hbm, wu_hbm, wd_hbm), (wg_ref, wu_ref, wd_ref),
                          stage_cols, stage_rows, sems)

    def stage_out_projection():
        halves = [pltpu.make_async_copy(wo_hbm.at[layer, pl.ds(s * SGU_WIDTH, SGU_WIDTH), :],
                                        stage_rows.at[s], sems.at[s, 2]) for s in range(2)]
        for copy in halves + weights.copies(0, (0, 1)) + weights.copies(1, (0, 1)):
            copy.start()
        for s, copy in enumerate(halves):
            copy.wait()
            wo_ref[s * SGU_WIDTH:(s + 1) * SGU_WIDTH, :] = stage_rows[s].astype(bf16)
        for copy in weights.copies(0, (2,)) + weights.copies(1, (2,)):
            copy.start()

    def step(landing):
        if landing:
            stage_out_projection()
        x = jnp.concatenate(
            [x_ref[r0:r0 + FFN_NORM_ROWS]
             + _dot_tn(yat_ref[0, :, r0:r0 + FFN_NORM_ROWS], wo_ref[:SGU_WIDTH, :])
             + _dot_tn(ybt_ref[0, :, r0:r0 + FFN_NORM_ROWS], wo_ref[SGU_WIDTH:, :])
             for r0 in range(0, x_ref.shape[0], FFN_NORM_ROWS)], axis=0)
        x = _swiglu_half_step(x, gain_ref[...], weights, landing)
        if final_norm:
            x = _rmsnorm_rows(x, fgain_ref[...])
        o_ref[...] = x

    pl.when(first_step)(functools.partial(step, True))
    pl.when(jnp.logical_not(first_step))(functools.partial(step, False))


def _const_spec(shape):
    nd = len(shape)
    return pl.BlockSpec(shape, lambda *_: (0,) * nd, pipeline_mode=pl.Buffered(1))


def _layer_spec(layer, shape):
    nd = len(shape)
    return pl.BlockSpec((None,) + tuple(shape), lambda *_: (layer,) + (0,) * nd,
                        pipeline_mode=pl.Buffered(1))


_HBM_SPEC = pl.BlockSpec(memory_space=pl.ANY)


def _ffn_weight_scratch():
    chunk_major = pltpu.VMEM((len(FFN_CHUNKS), D_MODEL, FFN_CHUNK_MAX), bf16)
    return [chunk_major, chunk_major,
            pltpu.VMEM((D_FF, D_MODEL), bf16),
            pltpu.VMEM((2, 2, D_MODEL, FFN_CHUNK_MAX), f32),
            pltpu.VMEM((2, FFN_CHUNK_MAX, D_MODEL), f32),
            pltpu.SemaphoreType.DMA((2, 3))]


def _ffn_call(layer, x2d, gain, wg, wu, wd):
    t = x2d.shape[0]
    tm = FFN_TOKENS
    row_spec = pl.BlockSpec((tm, D_MODEL), lambda i: (i, 0))
    return pl.pallas_call(
        functools.partial(_ffn_kernel, layer=layer),
        grid=(t // tm,),
        in_specs=[row_spec, _layer_spec(layer, (1, D_MODEL)), _HBM_SPEC, _HBM_SPEC, _HBM_SPEC],
        out_specs=row_spec,
        out_shape=jax.ShapeDtypeStruct((t, D_MODEL), f32),
        scratch_shapes=_ffn_weight_scratch(),
        compiler_params=pltpu.CompilerParams(
            dimension_semantics=("arbitrary",), vmem_limit_bytes=VMEM_LIMIT_BYTES),
        name="ffn",
    )(x2d, gain, wg, wu, wd)


def _mix_ffn_call(layer, x2d, yat, ybt, wo, gain, wg, wu, wd, fgain, final_norm):
    t = x2d.shape[0]
    tm = FFN_TOKENS
    tiles_per_seq = ybt.shape[2] // tm
    row_spec = pl.BlockSpec((tm, D_MODEL), lambda i: (i, 0))
    cm_spec = pl.BlockSpec((1, SGU_WIDTH, tm),
                           lambda i: (i // tiles_per_seq, 0, i % tiles_per_seq))
    return pl.pallas_call(
        functools.partial(_mix_ffn_kernel, layer=layer, final_norm=final_norm),
        grid=(t // tm,),
        in_specs=[row_spec, cm_spec, cm_spec, _HBM_SPEC, _layer_spec(layer, (1, D_MODEL)),
                  _HBM_SPEC, _HBM_SPEC, _HBM_SPEC, _const_spec((1, D_MODEL))],
        out_specs=row_spec,
        out_shape=jax.ShapeDtypeStruct((t, D_MODEL), f32),
        scratch_shapes=[pltpu.VMEM((2 * SGU_WIDTH, D_MODEL), bf16)] + _ffn_weight_scratch(),
        compiler_params=pltpu.CompilerParams(
            dimension_semantics=("arbitrary",), vmem_limit_bytes=VMEM_LIMIT_BYTES),
        name="mix_ffn",
    )(x2d, yat, ybt, wo, gain, wg, wu, wd, fgain)


def _gelu_exact(x):
    return 0.5 * x * (1.0 + lax.erf(x * (1.0 / math.sqrt(2.0))))


_COL_ZV, _COL_ZU, _COL_Q, _COL_K, _COL_V = (
    SGU_WIDTH, 0, 2 * SGU_WIDTH, 2 * SGU_WIDTH + DIFF_WIDTH, 2 * SGU_WIDTH + 2 * DIFF_WIDTH)
INPROJ_COL_ORDER = (_COL_ZV, _COL_ZU, _COL_Q, _COL_K, _COL_V)
INPROJ_COL_WIDTH = 512
assert SGU_WIDTH == DIFF_WIDTH == INPROJ_COL_WIDTH


def _inproj_kernel(x_ref, gain_ref, win_hbm, sgain_ref, ws_ref, bs_ref,
                   yat_ref, qt_ref, k_ref, vt_ref, win_ref, stage_ref, sems, *, layer):
    tm = x_ref.shape[0]
    width = INPROJ_COL_WIDTH
    first_step = pl.program_id(0) == 0

    def copy(n):
        return pltpu.make_async_copy(
            win_hbm.at[layer, :, pl.ds(INPROJ_COL_ORDER[n], width)],
            stage_ref.at[n % 2], sems.at[n % 2])

    def step(landing):
        def columns(n):
            if landing:
                copy(n).wait()
                win_ref[n] = stage_ref[n % 2].astype(bf16)
                if n + 2 < len(INPROJ_COL_ORDER):
                    copy(n + 2).start()
            return win_ref[n]

        if landing:
            copy(0).start()
            copy(1).start()
        h = _rmsnorm_rows(x_ref[...], gain_ref[...]).astype(bf16)

        n_chunks = tm // CHUNK
        v = _gelu_exact(_dot_nt_t(columns(0), h))
        zu = _dot_nt_t(columns(1), h)
        stacked = []
        for g in range(SGU_GROUPS):
            lo, hi = g * SGU_GROUP_DIM, (g + 1) * SGU_GROUP_DIM
            vg = v[lo:hi]
            ms = jnp.mean(vg * vg, axis=0, keepdims=True)
            vn = (vg * lax.rsqrt(ms + EPS) * sgain_ref[lo:hi]).astype(bf16)
            stacked.append(jnp.concatenate(
                [vn[:, c * CHUNK:(c + 1) * CHUNK] for c in range(n_chunks)], axis=0))
        qt = _dot_nt_t(columns(2), h)
        u = _gelu_exact(zu)

        row = lax.broadcasted_iota(jnp.int32, (CHUNK, CHUNK), 0)
        col = lax.broadcasted_iota(jnp.int32, (CHUNK, CHUNK), 1)
        causal = col <= row
        gates = []
        for g in range(SGU_GROUPS):
            w = jnp.where(causal, ws_ref[g], 0.0).astype(bf16)
            gates.append(_dot_nt(stacked[g], w) + bs_ref[g])
        k = _dot(h, columns(3))
        qt_ref[0] = (qt * (DIFF_QK_DIM ** -0.5 * LOG2_E)).astype(bf16)
        for g in range(SGU_GROUPS):
            lo, hi = g * SGU_GROUP_DIM, (g + 1) * SGU_GROUP_DIM
            for c in range(n_chunks):
                gate = gates[g][c * SGU_GROUP_DIM:(c + 1) * SGU_GROUP_DIM]
                yat_ref[0, lo:hi, c * CHUNK:(c + 1) * CHUNK] = (
                    u[lo:hi, c * CHUNK:(c + 1) * CHUNK] * gate).astype(bf16)

        vt = _dot_nt_t(columns(4), h)
        for hd in range(DIFF_HEADS):
            k_ref[0, hd] = k[:, hd * DIFF_V_DIM:(hd + 1) * DIFF_V_DIM].astype(bf16)
        vt = vt.astype(bf16)
        ones_rows = jnp.ones((VT_ROWS - DIFF_V_DIM, ATTN_BLOCK), bf16)
        for hd in range(DIFF_HEADS):
            for c in range(tm // ATTN_BLOCK):
                vt_ref[0, hd, c, :DIFF_V_DIM, :] = vt[hd * DIFF_V_DIM:(hd + 1) * DIFF_V_DIM,
                                                      c * ATTN_BLOCK:(c + 1) * ATTN_BLOCK]
                vt_ref[0, hd, c, DIFF_V_DIM:, :] = ones_rows

    pl.when(first_step)(functools.partial(step, True))
    pl.when(jnp.logical_not(first_step))(functools.partial(step, False))


def _inproj_call(layer, x2d, seq, gain, win, sgain, ws, bs):
    t = x2d.shape[0]
    tm = INPROJ_TOKENS
    tiles_per_seq = seq // tm
    row_spec = pl.BlockSpec((tm, D_MODEL), lambda i: (i, 0))
    cm_spec = pl.BlockSpec((1, SGU_WIDTH, tm), lambda i: (i // tiles_per_seq, 0, i % tiles_per_seq))
    cm_shape = jax.ShapeDtypeStruct((t // seq, SGU_WIDTH, seq), bf16)
    vt_spec = pl.BlockSpec((1, DIFF_HEADS, tm // ATTN_BLOCK, VT_ROWS, ATTN_BLOCK),
                           lambda i: (i // tiles_per_seq, 0, i % tiles_per_seq, 0, 0))
    vt_shape = jax.ShapeDtypeStruct(
        (t // seq, DIFF_HEADS, seq // ATTN_BLOCK, VT_ROWS, ATTN_BLOCK), bf16)
    return pl.pallas_call(
        functools.partial(_inproj_kernel, layer=layer),
        grid=(t // tm,),
        in_specs=[row_spec, _layer_spec(layer, (1, D_MODEL)), _HBM_SPEC,
                  _layer_spec(layer, (SGU_WIDTH, 1)),
                  _layer_spec(layer, (SGU_GROUPS, CHUNK, CHUNK)),
                  _layer_spec(layer, (SGU_GROUPS, 1, CHUNK))],
        out_specs=[cm_spec, cm_spec,
                   pl.BlockSpec((1, DIFF_HEADS, tm, DIFF_V_DIM),
                                lambda i: (i // tiles_per_seq, 0, i % tiles_per_seq, 0)),
                   vt_spec],
        out_shape=[cm_shape, cm_shape,
                   jax.ShapeDtypeStruct((t // seq, DIFF_HEADS, seq, DIFF_V_DIM), bf16), vt_shape],
        scratch_shapes=[pltpu.VMEM((len(INPROJ_COL_ORDER), D_MODEL, INPROJ_COL_WIDTH), bf16),
                        pltpu.VMEM((2, D_MODEL, INPROJ_COL_WIDTH), f32),
                        pltpu.SemaphoreType.DMA((2,))],
        compiler_params=pltpu.CompilerParams(
            dimension_semantics=("arbitrary",), vmem_limit_bytes=VMEM_LIMIT_BYTES),
        name="inproj_sgu",
    )(x2d, gain, win, sgain, ws, bs)


def _attn_kernel(lq1_ref, lk1_ref, lq2_ref, lk2_ref, gain_ref, kbias_ref, qt_ref, k_ref, vt_ref,
                 o_ref, qop_ref, m_ref, acc_ref, *, lam_init):
    blk = ATTN_BLOCK
    group = pl.program_id(1)
    first_block = group * ATTN_GROUP
    heads = range(DIFF_HEADS)
    slopes = [2.0 ** (-8.0 * (hd + 1) / DIFF_HEADS) * LOG2_E for hd in heads]
    krow = lax.broadcasted_iota(jnp.int32, (blk, blk), 0)
    qcol = lax.broadcasted_iota(jnp.int32, (blk, blk), 1)
    causal = krow <= qcol

    row = lax.broadcasted_iota(jnp.int32, (DIFF_V_DIM, 2 * blk), 0)
    c1 = jnp.asarray(LOG2_E, f32).astype(bf16).astype(f32)
    q_tail = jnp.where(row == 0, c1, jnp.where(row == 1, LOG2_E - c1, 0.0)).astype(bf16)
    q_zero = jnp.zeros((DIFF_QK_DIM, blk), bf16)
    for qs in range(ATTN_GROUP):
        for hd in heads:
            qt = qt_ref[0, hd * DIFF_V_DIM:(hd + 1) * DIFF_V_DIM, qs * blk:(qs + 1) * blk]
            qop_ref[qs, hd, :DIFF_QK_DIM, :blk] = qt[:DIFF_QK_DIM]
            qop_ref[qs, hd, :DIFF_QK_DIM, blk:] = q_zero
            qop_ref[qs, hd, DIFF_QK_DIM:DIFF_V_DIM, :blk] = q_zero
            qop_ref[qs, hd, DIFF_QK_DIM:DIFF_V_DIM, blk:] = qt[DIFF_QK_DIM:]
            qop_ref[qs, hd, DIFF_V_DIM:, :] = q_tail

    def scores(item):
        qs, kb, hd, mp, diagonal = item
        k_rows = k_ref[0, hd, pl.ds(pl.multiple_of(kb * blk, blk), blk), :]
        s = _dot(jnp.concatenate([k_rows, kbias_ref[hd]], axis=1),
                 qop_ref[qs, hd, :, mp * blk:(mp + 1) * blk])
        if diagonal:
            s = jnp.where(causal, s, NEG_INF)
        return s

    def softmax(item, s):
        qs, kb, hd, mp, diagonal = item
        lanes = slice(mp * blk, (mp + 1) * blk)
        s_max = jnp.max(s, axis=0, keepdims=True)
        if diagonal:
            m_new = s_max
            p, alpha = jnp.exp2(s - m_new), None
        else:
            beta = slopes[hd] * ((kb - (first_block + qs)) * blk).astype(f32)
            m_old = m_ref[qs, hd, :, lanes]
            m_new = jnp.maximum(m_old, s_max + beta)
            p, alpha = jnp.exp2(s - (m_new - beta)), jnp.exp2(m_old - m_new)
        m_ref[qs, hd, :, lanes] = m_new
        return item, p.astype(bf16), alpha

    def accumulate(item, p, alpha):
        qs, kb, hd, mp, _ = item
        lanes = slice(mp * blk, (mp + 1) * blk)
        pv = _dot(vt_ref[0, hd, kb], p)
        acc_ref[qs, hd, :, lanes] = (
            pv if alpha is None else alpha * acc_ref[qs, hd, :, lanes] + pv)

    def run(items):
        scored = [scores(item) for item in items[:ATTN_LOOKAHEAD]]
        weighted = []
        for n, item in enumerate(items):
            if n + ATTN_LOOKAHEAD < len(items):
                scored.append(scores(items[n + ATTN_LOOKAHEAD]))
            weighted.append(softmax(item, scored.pop(0)))
            if len(weighted) > ATTN_PV_DELAY:
                accumulate(*weighted.pop(0))
        for args in weighted:
            accumulate(*args)

    diagonal_region = []
    for rnd in range(ATTN_GROUP):
        for qs in range(rnd, ATTN_GROUP):
            key = qs if rnd == 0 else rnd - 1
            diagonal_region += [(qs, first_block + key, hd, mp, rnd == 0)
                                for hd in heads for mp in range(2)]
    run(diagonal_region)

    def earlier_keys(j, carry):
        run([(qs, 2 * j + kbo, hd, mp, False)
             for kbo in range(2) for qs in range(ATTN_GROUP) for hd in heads for mp in range(2)])
        return carry

    lax.fori_loop(0, first_block // 2, earlier_keys, 0)

    lam = (jnp.exp(jnp.sum(lq1_ref[...] * lk1_ref[...], axis=-1, keepdims=True))
           - jnp.exp(jnp.sum(lq2_ref[...] * lk2_ref[...], axis=-1, keepdims=True))
           + lam_init)
    gain = gain_ref[...] * (1.0 - lam_init)
    for qs in range(ATTN_GROUP):
        for hd in heads:
            acc1, acc2 = acc_ref[qs, hd, :, :blk], acc_ref[qs, hd, :, blk:]
            o = (acc1[:DIFF_V_DIM] / acc1[DIFF_V_DIM:DIFF_V_DIM + 1]
                 - lam * (acc2[:DIFF_V_DIM] / acc2[DIFF_V_DIM:DIFF_V_DIM + 1]))
            ms = jnp.mean(o * o, axis=0, keepdims=True)
            o_ref[0, hd * DIFF_V_DIM:(hd + 1) * DIFF_V_DIM, qs * blk:(qs + 1) * blk] = (
                o * lax.rsqrt(ms + EPS) * gain).astype(bf16)


def _attn_call(layer, lq1, lk1, lq2, lk2, gain, kbias, qt, k, vt, lam_init):
    b, _, s, _ = k.shape
    blk = ATTN_BLOCK
    lam_spec = _layer_spec(layer, (1, DIFF_QK_DIM))
    tq = ATTN_GROUP * blk
    q_spec = pl.BlockSpec((1, DIFF_WIDTH, tq), lambda bi, gi: (bi, 0, gi))
    k_spec = pl.BlockSpec((1, DIFF_HEADS, s, DIFF_V_DIM), lambda bi, gi: (bi, 0, 0, 0))
    vt_spec = pl.BlockSpec((1, DIFF_HEADS, s // blk, VT_ROWS, blk),
                           lambda bi, gi: (bi, 0, 0, 0, 0))
    return pl.pallas_call(
        functools.partial(_attn_kernel, lam_init=lam_init),
        grid=(b, s // tq),
        in_specs=[lam_spec, lam_spec, lam_spec, lam_spec, _layer_spec(layer, (DIFF_V_DIM, 1)),
                  _const_spec((DIFF_HEADS, blk, DIFF_V_DIM)), q_spec, k_spec, vt_spec],
        out_specs=pl.BlockSpec((1, DIFF_WIDTH, tq), lambda bi, gi: (bi, 0, gi)),
        out_shape=jax.ShapeDtypeStruct((b, DIFF_WIDTH, s), bf16),
        scratch_shapes=[pltpu.VMEM((ATTN_GROUP, DIFF_HEADS, 2 * DIFF_V_DIM, 2 * blk), bf16),
                        pltpu.VMEM((ATTN_GROUP, DIFF_HEADS, 1, 2 * blk), f32),
                        pltpu.VMEM((ATTN_GROUP, DIFF_HEADS, VT_ROWS, 2 * blk), f32)],
        compiler_params=pltpu.CompilerParams(
            dimension_semantics=("parallel", "arbitrary"),
            vmem_limit_bytes=VMEM_LIMIT_BYTES),
        name="diff_attn",
    )(lq1, lk1, lq2, lk2, gain, kbias, qt, k, vt)


def _alibi_key_columns():
    r = jnp.arange(ATTN_BLOCK, dtype=f32)
    slopes = jnp.exp2(-8.0 * jnp.arange(1, DIFF_HEADS + 1, dtype=f32) / DIFF_HEADS)
    cols = jnp.zeros((DIFF_HEADS, ATTN_BLOCK, DIFF_V_DIM), f32)
    cols = cols.at[:, :, 0:2].set((slopes[:, None] * r[None, :])[:, :, None])
    return cols.astype(bf16)


def kernel(x, ffn1_norm, ffn1_w_gate, ffn1_w_up, ffn1_w_down, mix_norm, w_in, sgu_norm, sgu_w, sgu_b, lambda_q1, lambda_k1, lambda_q2, lambda_k2, diff_subln, w_out, ffn2_norm, ffn2_w_gate, ffn2_w_up, ffn2_w_down, final_norm):
    b, s, d = x.shape
    depth = w_in.shape[0]
    t = b * s
    x2d = x.reshape(t, d)
    kbias = _alibi_key_columns()

    as_rows = lambda p: p.reshape(depth, 1, -1)
    wg1, wu1, wd1 = ffn1_w_gate, ffn1_w_up, ffn1_w_down
    wg2, wu2, wd2 = ffn2_w_gate, ffn2_w_up, ffn2_w_down
    win = w_in
    wo = w_out
    sgu_norm_cols = sgu_norm.reshape(depth, SGU_WIDTH, 1)
    sgu_b_rows = sgu_b.reshape(depth, SGU_GROUPS, 1, CHUNK)
    subln_cols = diff_subln.reshape(depth, DIFF_V_DIM, 1)

    for l in range(depth):
        x2d = _ffn_call(l, x2d, as_rows(ffn1_norm), wg1, wu1, wd1)
        yat, qt, k, vt = _inproj_call(l, x2d, s, as_rows(mix_norm), win,
                                      sgu_norm_cols, sgu_w, sgu_b_rows)
        lam_init = 0.8 - 0.6 * math.exp(-0.3 * l)
        ybt = _attn_call(l, as_rows(lambda_q1), as_rows(lambda_k1), as_rows(lambda_q2),
                         as_rows(lambda_k2), subln_cols, kbias,
                         qt, k, vt, lam_init)
        x2d = _mix_ffn_call(l, x2d, yat, ybt, wo, as_rows(ffn2_norm),
                            wg2, wu2, wd2, final_norm.reshape(1, d),
                            final_norm=(l == depth - 1))
    return x2d.reshape(b, s, d)
```
